```python
import math
import jax, jax.numpy as jnp
from jax import lax
import numpy as np

D_MODEL = 1024
BATCH = 8
SEQ = 4096
DEPTH = 1

MEM_LEN = 256
ATT_WIDTH = D_MODEL // 2
HEAD_DIM = 64
N_ATT_HEADS = ATT_WIDTH // HEAD_DIM
CONV_WIDTH = D_MODEL - ATT_WIDTH
CONV_K = 3
DILATED_PATTERNS = ((128, 1), (512, 4), (2048, 16))
N_MEM_HEADS = 4
MEM_HEAD_DIM = D_MODEL // N_MEM_HEADS
D_FF = 4 * D_MODEL
NORM_EPS = 1e-6
NEG_INF = -1e30
IN_COLS = 3 * ATT_WIDTH + 3 * CONV_WIDTH

kernel_name = "hybrid_dilated_attn_shortconv_block"


def rms_norm(x, g):
    xf = x.astype(jnp.float32)
    y = xf * lax.rsqrt(jnp.mean(xf * xf, axis=-1, keepdims=True) + NORM_EPS)
    return (y * g.astype(jnp.float32)).astype(x.dtype)


def dilated_window_attention(q, k, v, window, dilation):
    b, s, h, e = q.shape
    steps = window // dilation
    span = steps * dilation
    s_pad = -(-s // span) * span
    nb = s_pad // span
    pad = ((0, 0), (0, s_pad - s), (0, 0), (0, 0))

    def blocks(t):
        return jnp.pad(t, pad).reshape(b, nb, steps, dilation, h, e)

    def with_prev(t):
        prev = jnp.pad(t, ((0, 0), (1, 0), (0, 0), (0, 0), (0, 0), (0, 0)))[:, :-1]
        return jnp.concatenate([prev, t], axis=2)

    qb = blocks(q)
    kk = with_prev(blocks(k))
    vv = with_prev(blocks(v))
    scale = 1.0 / math.sqrt(e)
    scores = jnp.einsum('bnqrhe,bnkrhe->bnrhqk', qb, kk).astype(jnp.float32) * scale
    i = jnp.arange(steps)[:, None]
    j = jnp.arange(2 * steps)[None, :]
    band = (j >= i) & (j <= i + steps)
    has_prev = (jnp.arange(nb) > 0)[:, None, None]
    valid = band[None] & (has_prev | (j[None] >= steps))
    scores = jnp.where(valid[None, :, None, None], scores, NEG_INF)
    lse = jax.nn.logsumexp(scores, axis=-1)
    p = jnp.exp(scores - lse[..., None]).astype(v.dtype)
    o = jnp.einsum('bnrhqk,bnkrhe->bnqrhe', p, vv)
    o = o.reshape(b, s_pad, h, e)[:, :s]
    lse = jnp.transpose(lse, (0, 1, 4, 2, 3)).reshape(b, s_pad, h)[:, :s]
    return o, lse


def short_gated_conv(bg, cg, xc, conv_w):
    u = cg * xc
    up = jnp.pad(u, ((0, 0), (CONV_K - 1, 0), (0, 0)))
    s = u.shape[1]
    conv = sum(up[:, tap:tap + s] * conv_w[tap] for tap in range(CONV_K))
    return bg * conv


def hybrid_mixer(h, w_in, conv_w, g_attn_out, g_conv_out, w_out):
    b, s, _ = h.shape
    proj = h @ w_in
    q, k, v, bg, cg, xc = jnp.split(proj, 6, axis=-1)
    q = q.reshape(b, s, N_ATT_HEADS, HEAD_DIM)
    k = k.reshape(b, s, N_ATT_HEADS, HEAD_DIM)
    v = v.reshape(b, s, N_ATT_HEADS, HEAD_DIM)
    outs, lses = [], []
    for window, dilation in DILATED_PATTERNS:
        o, lse = dilated_window_attention(q, k, v, window, dilation)
        outs.append(o)
        lses.append(lse)
    mix_w = jax.nn.softmax(jnp.stack(lses, axis=0), axis=0)
    attn = jnp.einsum('pbsh,pbshe->bshe', mix_w, jnp.stack(outs, axis=0).astype(jnp.float32))
    attn = attn.astype(h.dtype).reshape(b, s, ATT_WIDTH)
    conv = short_gated_conv(bg, cg, xc, conv_w)
    merged = jnp.concatenate([rms_norm(attn, g_attn_out), rms_norm(conv, g_conv_out)], axis=-1)
    return merged @ w_out


def memory_cross_attention(h, mem_n, w_q_mem, w_kv_mem, w_o_mem):
    b, s, _ = h.shape
    q = (h @ w_q_mem).reshape(b, s, N_MEM_HEADS, MEM_HEAD_DIM)
    kv = mem_n @ w_kv_mem
    k, v = jnp.split(kv, 2, axis=-1)
    k = k.reshape(b, MEM_LEN, N_MEM_HEADS, MEM_HEAD_DIM)
    v = v.reshape(b, MEM_LEN, N_MEM_HEADS, MEM_HEAD_DIM)
    scores = jnp.einsum('bshe,bmhe->bhsm', q, k).astype(jnp.float32) / math.sqrt(MEM_HEAD_DIM)
    p = jax.nn.softmax(scores, axis=-1).astype(v.dtype)
    o = jnp.einsum('bhsm,bmhe->bshe', p, v).reshape(b, s, D_MODEL)
    return o @ w_o_mem


def squared_relu_mlp(h, w_up, w_down):
    a = jax.nn.relu(h @ w_up)
    return (a * a) @ w_down


def setup_inputs(seed: int = 0) -> dict:
    key = jax.random.key(seed)
    ks = jax.random.split(key, 17)
    f32 = jnp.float32

    def dense(k, fan_in, shape, gain=1.0):
        return jax.random.normal(k, shape, f32) * (gain * fan_in ** -0.5)

    def gain(k, n):
        return 1.0 + 0.02 * jax.random.normal(k, (n,), f32)

    return {
        "x": jax.random.normal(ks[0], (BATCH, SEQ, D_MODEL), f32),
        "mem": jax.random.normal(ks[1], (BATCH, MEM_LEN, D_MODEL), f32),
        "g_mix": gain(ks[2], D_MODEL),
        "w_in": dense(ks[3], D_MODEL, (D_MODEL, IN_COLS)),
        "conv_w": dense(ks[4], CONV_K, (CONV_K, CONV_WIDTH)),
        "g_attn_out": gain(ks[5], ATT_WIDTH),
        "g_conv_out": gain(ks[6], CONV_WIDTH),
        "w_out": dense(ks[7], D_MODEL, (D_MODEL, D_MODEL), 0.5),
        "g_xattn": gain(ks[8], D_MODEL),
        "g_mem": gain(ks[9], D_MODEL),
        "w_q_mem": dense(ks[10], D_MODEL, (D_MODEL, D_MODEL)),
        "w_kv_mem": dense(ks[11], D_MODEL, (D_MODEL, 2 * D_MODEL)),
        "w_o_mem": dense(ks[12], D_MODEL, (D_MODEL, D_MODEL), 0.5),
        "g_mlp": gain(ks[13], D_MODEL),
        "w_up": dense(ks[14], D_MODEL, (D_MODEL, D_FF)),
        "w_down": dense(ks[15], D_FF, (D_FF, D_MODEL), 0.5),
        "g_final": gain(ks[16], D_MODEL),
    }


def reference(x, mem, g_mix, w_in, conv_w, g_attn_out, g_conv_out, w_out,
              g_xattn, g_mem, w_q_mem, w_kv_mem, w_o_mem,
              g_mlp, w_up, w_down, g_final):
    for _ in range(DEPTH):
        x = x + hybrid_mixer(rms_norm(x, g_mix), w_in, conv_w, g_attn_out, g_conv_out, w_out)
        x = x + memory_cross_attention(rms_norm(x, g_xattn), rms_norm(mem, g_mem),
                                       w_q_mem, w_kv_mem, w_o_mem)
        x = x + squared_relu_mlp(rms_norm(x, g_mlp), w_up, w_down)
    return rms_norm(x, g_final)
```

```python
import functools
import math

import jax
import jax.numpy as jnp
from jax.experimental import pallas as pl
from jax.experimental.pallas import tpu as pltpu

D_MODEL = 1024
ATT_WIDTH = 512
HEAD_DIM = 64
N_ATT_HEADS = 8
CONV_WIDTH = 512
CONV_K = 3
DILATED_PATTERNS = ((128, 1), (512, 4), (2048, 16))
BAND_STEPS = 128
N_MEM_HEADS = 4
MEM_HEAD_DIM = 256
D_FF = 4096
NORM_EPS = 1e-6
NEG_INF = -1e30
LSE_LANES = 128
LSE_REP = LSE_LANES // N_ATT_HEADS

F32_SUBLANES = 8
VMEM_LIMIT_BYTES = 56 * 1024 * 1024

TOKEN_TILE = 512
ATTN_SUBTILES = 4


def _rms(x, g):
    ms = jnp.mean(x * x, axis=-1, keepdims=True)
    return x * jax.lax.rsqrt(ms + NORM_EPS) * g


def _dot(a, b):
    return jnp.dot(a, b, preferred_element_type=jnp.float32)


def _dot_nt(a, b):
    return jax.lax.dot_general(a, b, (((1,), (1,)), ((), ())),
                               preferred_element_type=jnp.float32)


def _params(n_axes):
    return pltpu.CompilerParams(
        dimension_semantics=("arbitrary",) * n_axes,
        vmem_limit_bytes=VMEM_LIMIT_BYTES)


def _resident(shape):
    return pl.BlockSpec(shape, lambda *_: (0,) * len(shape))


def _in_proj_kernel(tiles_per_seq, x_ref, g_ref, w_ref, cw_ref, gc_ref,
                    q_ref, k_ref, v_ref, c_ref, ubuf_ref):
    tm = x_ref.shape[0]
    h = _rms(x_ref[...], g_ref[...]).astype(jnp.bfloat16)

    def proj(col):
        return _dot(h, w_ref[:, col * ATT_WIDTH:(col + 1) * ATT_WIDTH])

    q_ref[...] = (proj(0) * (1.0 / math.sqrt(HEAD_DIM))).astype(jnp.bfloat16)
    k_ref[...] = proj(1).astype(jnp.bfloat16)
    v_ref[...] = proj(2).astype(jnp.bfloat16)

    @pl.when(pl.program_id(0) % tiles_per_seq == 0)
    def _():
        ubuf_ref[tm:tm + F32_SUBLANES, :] = jnp.zeros((F32_SUBLANES, CONV_WIDTH), jnp.float32)

    ubuf_ref[0:F32_SUBLANES, :] = ubuf_ref[tm:tm + F32_SUBLANES, :]
    u = proj(4) * proj(5)
    ubuf_ref[F32_SUBLANES:tm + F32_SUBLANES, :] = u
    conv = (u * cw_ref[2:3, :]
            + ubuf_ref[F32_SUBLANES - 1:tm + F32_SUBLANES - 1, :] * cw_ref[1:2, :]
            + ubuf_ref[F32_SUBLANES - 2:tm + F32_SUBLANES - 2, :] * cw_ref[0:1, :])
    y = proj(3) * conv
    c_ref[...] = _rms(y, gc_ref[...]).astype(jnp.bfloat16)


def _in_proj(x2d, g_mix, w_in, conv_w, g_conv_out, seq_len):
    t = x2d.shape[0]
    tm = TOKEN_TILE
    assert seq_len % tm == 0 and t % tm == 0
    out = jax.ShapeDtypeStruct((t, ATT_WIDTH), jnp.bfloat16)
    row_spec = pl.BlockSpec((tm, ATT_WIDTH), lambda i: (i, 0))
    return pl.pallas_call(
        functools.partial(_in_proj_kernel, seq_len // tm),
        grid=(t // tm,),
        in_specs=[
            pl.BlockSpec((tm, D_MODEL), lambda i: (i, 0)),
            _resident((1, D_MODEL)),
            _resident(w_in.shape),
            _resident(conv_w.shape),
            _resident((1, CONV_WIDTH)),
        ],
        out_specs=[row_spec, row_spec, row_spec, row_spec],
        out_shape=[out, out, out, out],
        scratch_shapes=[pltpu.VMEM((tm + F32_SUBLANES, CONV_WIDTH), jnp.float32)],
        compiler_params=_params(1),
        name="in_proj",
    )(x2d, g_mix.reshape(1, -1), w_in, conv_w, g_conv_out.reshape(1, -1))


def _attn_kernel(row_blocks, lane_blocks, q_ref, kc_ref, kp_ref, vc_ref, vp_ref,
                 o_ref, lse_ref):
    s = BAND_STEPS
    has_prev = pl.program_id(1) > 0
    qi = jax.lax.broadcasted_iota(jnp.int32, (s, 2 * s), 0)
    kj = jax.lax.broadcasted_iota(jnp.int32, (s, 2 * s), 1)
    band = (kj >= qi) & (kj <= qi + s)
    valid_first = band & ((kj >= s) | has_prev)

    for i in range(row_blocks):
        rows = slice(i * s, (i + 1) * s)
        for r in range(lane_blocks):
            for hd in range(N_ATT_HEADS):
                lanes = slice(r * ATT_WIDTH + hd * HEAD_DIM,
                              r * ATT_WIDTH + (hd + 1) * HEAD_DIM)
                q = q_ref[rows, lanes]
                if i == 0:
                    k_prev, v_prev, valid = kp_ref[:, lanes], vp_ref[:, lanes], valid_first
                else:
                    prev_rows = slice((i - 1) * s, i * s)
                    k_prev, v_prev, valid = kc_ref[prev_rows, lanes], vc_ref[prev_rows, lanes], band
                k = jnp.concatenate([k_prev, kc_ref[rows, lanes]], axis=0)
                v = jnp.concatenate([v_prev, vc_ref[rows, lanes]], axis=0)
                sc = jnp.where(valid, _dot_nt(q, k), NEG_INF)
                m = jnp.max(sc, axis=-1, keepdims=True)
                p = jnp.exp(sc - m)
                l = jnp.sum(p, axis=-1, keepdims=True)
                o = _dot(p.astype(jnp.bfloat16), v) / l
                o_ref[rows, lanes] = o
                lse_lanes = slice(r * LSE_LANES + hd * LSE_REP,
                                  r * LSE_LANES + (hd + 1) * LSE_REP)
                lse_ref[rows, lse_lanes] = jnp.broadcast_to(m + jnp.log(l), (s, LSE_REP))


def _dilated_attention(q, k, v, dilation, row_blocks, lane_blocks):
    b, seq, _ = q.shape
    steps = seq // dilation
    tile_rows = row_blocks * BAND_STEPS
    assert steps % tile_rows == 0 and dilation % lane_blocks == 0
    view = lambda a: a.reshape(b, steps, dilation * ATT_WIDTH)
    width = lane_blocks * ATT_WIDTH
    cur = pl.BlockSpec((None, tile_rows, width), lambda bi, n, g: (bi, n, g))
    prev = pl.BlockSpec((None, BAND_STEPS, width),
                        lambda bi, n, g: (bi, jnp.maximum(n * row_blocks - 1, 0), g))
    o, lse = pl.pallas_call(
        functools.partial(_attn_kernel, row_blocks, lane_blocks),
        grid=(b, steps // tile_rows, dilation // lane_blocks),
        in_specs=[cur, cur, prev, cur, prev],
        out_specs=[cur, pl.BlockSpec((None, tile_rows, lane_blocks * LSE_LANES),
                                     lambda bi, n, g: (bi, n, g))],
        out_shape=[jax.ShapeDtypeStruct((b, steps, dilation * ATT_WIDTH), jnp.float32),
                   jax.ShapeDtypeStruct((b, steps, dilation * LSE_LANES), jnp.float32)],
        compiler_params=_params(3),
        name=f"dilated_attn_d{dilation}",
    )(view(q), view(k), view(k), view(v), view(v))
    return o.reshape(b * seq, ATT_WIDTH), lse.reshape(b * seq, LSE_LANES)


def _mix_out_kernel(o1_ref, o2_ref, o3_ref, l1_ref, l2_ref, l3_ref, c_ref, x_ref,
                    ga_ref, w_ref, y_ref):
    lses = [l1_ref[...], l2_ref[...], l3_ref[...]]
    top = jnp.maximum(jnp.maximum(lses[0], lses[1]), lses[2])
    es = [jnp.exp(l - top) for l in lses]
    den = es[0] + es[1] + es[2]
    ws = [e / den for e in es]
    heads = []
    for hd in range(N_ATT_HEADS):
        lanes = slice(hd * HEAD_DIM, (hd + 1) * HEAD_DIM)
        col = slice(hd * LSE_REP, hd * LSE_REP + 1)
        heads.append(ws[0][:, col] * o1_ref[:, lanes]
                     + ws[1][:, col] * o2_ref[:, lanes]
                     + ws[2][:, col] * o3_ref[:, lanes])
    attn = jnp.concatenate(heads, axis=-1)
    attn_n = _rms(attn, ga_ref[...]).astype(jnp.bfloat16)
    mixed = _dot(attn_n, w_ref[0:ATT_WIDTH, :]) + _dot(c_ref[...], w_ref[ATT_WIDTH:, :])
    y_ref[...] = x_ref[...] + mixed


def _mix_out(os, lses, convn, x2d, g_attn_out, w_out):
    t = x2d.shape[0]
    tm = TOKEN_TILE
    o_spec = pl.BlockSpec((tm, ATT_WIDTH), lambda i: (i, 0))
    l_spec = pl.BlockSpec((tm, LSE_LANES), lambda i: (i, 0))
    x_spec = pl.BlockSpec((tm, D_MODEL), lambda i: (i, 0))
    return pl.pallas_call(
        _mix_out_kernel,
        grid=(t // tm,),
        in_specs=[o_spec, o_spec, o_spec, l_spec, l_spec, l_spec, o_spec, x_spec,
                  _resident((1, ATT_WIDTH)), _resident(w_out.shape)],
        out_specs=x_spec,
        out_shape=jax.ShapeDtypeStruct((t, D_MODEL), jnp.float32),
        compiler_params=_params(1),
        name="mix_out_proj",
    )(*os, *lses, convn, x2d, g_attn_out.reshape(1, -1), w_out)


def _mem_kv_kernel(m_ref, g_ref, w_ref, kv_ref):
    h = _rms(m_ref[...], g_ref[...]).astype(jnp.bfloat16)
    kv_ref[...] = _dot(h, w_ref[...]).astype(jnp.bfloat16)


def _mem_kv(mem2d, g_mem, w_kv):
    rows = mem2d.shape[0]
    tm = TOKEN_TILE
    return pl.pallas_call(
        _mem_kv_kernel,
        grid=(rows // tm,),
        in_specs=[pl.BlockSpec((tm, D_MODEL), lambda i: (i, 0)),
                  _resident((1, D_MODEL)), _resident(w_kv.shape)],
        out_specs=pl.BlockSpec((tm, 2 * D_MODEL), lambda i: (i, 0)),
        out_shape=jax.ShapeDtypeStruct((rows, 2 * D_MODEL), jnp.bfloat16),
        compiler_params=_params(1),
        name="mem_kv_proj",
    )(mem2d, g_mem.reshape(1, -1), w_kv)


def _xattn_kernel(x_ref, g_ref, wq_ref, kv_ref, wo_ref, y_ref):
    x = x_ref[...]
    h = _rms(x, g_ref[...]).astype(jnp.bfloat16)
    heads = []
    for hd in range(N_MEM_HEADS):
        lanes = slice(hd * MEM_HEAD_DIM, (hd + 1) * MEM_HEAD_DIM)
        q = (_dot(h, wq_ref[:, lanes]) * (1.0 / math.sqrt(MEM_HEAD_DIM))).astype(jnp.bfloat16)
        k = kv_ref[:, lanes]
        v = kv_ref[:, D_MODEL + hd * MEM_HEAD_DIM:D_MODEL + (hd + 1) * MEM_HEAD_DIM]
        sc = _dot_nt(q, k)
        m = jnp.max(sc, axis=-1, keepdims=True)
        p = jnp.exp(sc - m)
        l = jnp.sum(p, axis=-1, keepdims=True)
        heads.append((_dot(p.astype(jnp.bfloat16), v) / l).astype(jnp.bfloat16))
    o = jnp.concatenate(heads, axis=-1)
    y_ref[...] = x + _dot(o, wo_ref[...])


def _xattn(x2d, g_xattn, w_q, kv, w_o, seq_len, mem_len):
    t = x2d.shape[0]
    tm = TOKEN_TILE
    tiles_per_seq = seq_len // tm
    x_spec = pl.BlockSpec((tm, D_MODEL), lambda i: (i, 0))
    return pl.pallas_call(
        _xattn_kernel,
        grid=(t // tm,),
        in_specs=[x_spec, _resident((1, D_MODEL)), _resident(w_q.shape),
                  pl.BlockSpec((mem_len, 2 * D_MODEL), lambda i: (i // tiles_per_seq, 0)),
                  _resident(w_o.shape)],
        out_specs=x_spec,
        out_shape=jax.ShapeDtypeStruct((t, D_MODEL), jnp.float32),
        compiler_params=_params(1),
        name="mem_xattn",
    )(x2d, g_xattn.reshape(1, -1), w_q, kv, w_o)


def _mlp_kernel(ff_chunk, x_ref, g_ref, wu_ref, wd_ref, gf_ref, y_ref):
    x = x_ref[...]
    h = _rms(x, g_ref[...]).astype(jnp.bfloat16)
    acc = x
    for c in range(D_FF // ff_chunk):
        cols = slice(c * ff_chunk, (c + 1) * ff_chunk)
        a = jnp.maximum(_dot(h, wu_ref[:, cols]), 0.0)
        acc = acc + _dot((a * a).astype(jnp.bfloat16), wd_ref[cols, :])
    y_ref[...] = _rms(acc, gf_ref[...])


def _mlp(x2d, g_mlp, w_up, w_down, g_final):
    t = x2d.shape[0]
    tm = TOKEN_TILE
    x_spec = pl.BlockSpec((tm, D_MODEL), lambda i: (i, 0))
    return pl.pallas_call(
        functools.partial(_mlp_kernel, 1024),
        grid=(t // tm,),
        in_specs=[x_spec, _resident((1, D_MODEL)), _resident(w_up.shape),
                  _resident(w_down.shape), _resident((1, D_MODEL))],
        out_specs=x_spec,
        out_shape=jax.ShapeDtypeStruct((t, D_MODEL), jnp.float32),
        compiler_params=_params(1),
        name="mlp_final_norm",
    )(x2d, g_mlp.reshape(1, -1), w_up, w_down, g_final.reshape(1, -1))


def kernel(x, mem, g_mix, w_in, conv_w, g_attn_out, g_conv_out, w_out, g_xattn, g_mem,
           w_q_mem, w_kv_mem, w_o_mem, g_mlp, w_up, w_down, g_final):
    b, seq, d = x.shape
    mem_len = mem.shape[1]
    bf = lambda w: w.astype(jnp.bfloat16)
    x2d = x.reshape(b * seq, d)

    q, k, v, convn = _in_proj(x2d, g_mix, bf(w_in), conv_w, g_conv_out, seq)
    q, k, v = (a.reshape(b, seq, ATT_WIDTH) for a in (q, k, v))
    os, lses = [], []
    for window, dilation in DILATED_PATTERNS:
        assert window // dilation == BAND_STEPS
        lane_blocks = min(dilation, ATTN_SUBTILES)
        o, lse = _dilated_attention(q, k, v, dilation, ATTN_SUBTILES // lane_blocks, lane_blocks)
        os.append(o)
        lses.append(lse)
    x1 = _mix_out(os, lses, convn, x2d, g_attn_out, bf(w_out))

    kv = _mem_kv(mem.reshape(b * mem_len, d), g_mem, bf(w_kv_mem))
    x2 = _xattn(x1, g_xattn, bf(w_q_mem), kv, bf(w_o_mem), seq, mem_len)
    y = _mlp(x2, g_mlp, bf(w_up), bf(w_down), g_final)
    return y.reshape(b, seq, d)
```

```python
import functools
import math

import jax
import jax.numpy as jnp
from jax.experimental import pallas as pl
from jax.experimental.pallas import tpu as pltpu

D_MODEL = 1024
ATT_WIDTH = 512
HEAD_DIM = 64
N_ATT_HEADS = 8
CONV_WIDTH = 512
CONV_K = 3
DILATED_PATTERNS = ((128, 1), (512, 4), (2048, 16))
BAND_STEPS = 128
N_MEM_HEADS = 4
MEM_HEAD_DIM = 256
D_FF = 4096
NORM_EPS = 1e-6
NEG_INF = -1e30
LSE_LANES = 128
LSE_REP = LSE_LANES // N_ATT_HEADS
PAIR_LANES = 2 * HEAD_DIM

F32_SUBLANES = 8
VMEM_LIMIT_BYTES = 56 * 1024 * 1024

TOKEN_TILE = 512
ATTN_SUBTILES = 4


def _rms(x, g):
    ms = jnp.mean(x * x, axis=-1, keepdims=True)
    return x * jax.lax.rsqrt(ms + NORM_EPS) * g


def _dot(a, b):
    return jnp.dot(a, b, preferred_element_type=jnp.float32)


def _dot_nt(a, b):
    return jax.lax.dot_general(a, b, (((1,), (1,)), ((), ())),
                               preferred_element_type=jnp.float32)


def _params(n_axes):
    return pltpu.CompilerParams(
        dimension_semantics=("arbitrary",) * n_axes,
        vmem_limit_bytes=VMEM_LIMIT_BYTES)


def _resident(shape):
    return pl.BlockSpec(shape, lambda *_: (0,) * len(shape))


def _in_proj_kernel(tiles_per_seq, x_ref, g_ref, w_ref, cw_ref, gc_ref,
                    q_ref, k_ref, v_ref, c_ref, ubuf_ref):
    tm = x_ref.shape[0]
    h = _rms(x_ref[...], g_ref[...]).astype(jnp.bfloat16)

    def proj(col):
        return _dot(h, w_ref[:, col * ATT_WIDTH:(col + 1) * ATT_WIDTH])

    q_ref[...] = (proj(0) * (1.0 / math.sqrt(HEAD_DIM))).astype(jnp.bfloat16)
    k_ref[...] = proj(1).astype(jnp.bfloat16)
    v_ref[...] = proj(2).astype(jnp.bfloat16)

    @pl.when(pl.program_id(0) % tiles_per_seq == 0)
    def _():
        ubuf_ref[tm:tm + F32_SUBLANES, :] = jnp.zeros((F32_SUBLANES, CONV_WIDTH), jnp.float32)

    ubuf_ref[0:F32_SUBLANES, :] = ubuf_ref[tm:tm + F32_SUBLANES, :]
    u = proj(4) * proj(5)
    ubuf_ref[F32_SUBLANES:tm + F32_SUBLANES, :] = u
    conv = (u * cw_ref[2:3, :]
            + ubuf_ref[F32_SUBLANES - 1:tm + F32_SUBLANES - 1, :] * cw_ref[1:2, :]
            + ubuf_ref[F32_SUBLANES - 2:tm + F32_SUBLANES - 2, :] * cw_ref[0:1, :])
    y = proj(3) * conv
    c_ref[...] = _rms(y, gc_ref[...]).astype(jnp.bfloat16)


def _in_proj(x2d, g_mix, w_in, conv_w, g_conv_out, seq_len):
    t = x2d.shape[0]
    tm = TOKEN_TILE
    assert seq_len % tm == 0 and t % tm == 0
    out = jax.ShapeDtypeStruct((t, ATT_WIDTH), jnp.bfloat16)
    row_spec = pl.BlockSpec((tm, ATT_WIDTH), lambda i: (i, 0))
    return pl.pallas_call(
        functools.partial(_in_proj_kernel, seq_len // tm),
        grid=(t // tm,),
        in_specs=[
            pl.BlockSpec((tm, D_MODEL), lambda i: (i, 0)),
            _resident((1, D_MODEL)),
            _resident(w_in.shape),
            _resident(conv_w.shape),
            _resident((1, CONV_WIDTH)),
        ],
        out_specs=[row_spec, row_spec, row_spec, row_spec],
        out_shape=[out, out, out, out],
        scratch_shapes=[pltpu.VMEM((tm + F32_SUBLANES, CONV_WIDTH), jnp.float32)],
        compiler_params=_params(1),
        name="in_proj",
    )(x2d, g_mix.reshape(1, -1), w_in, conv_w, g_conv_out.reshape(1, -1))


def _attn_kernel(row_blocks, lane_blocks, q_ref, kc_ref, kp_ref, vc_ref, vp_ref,
                 o_ref, lse_ref):
    s = BAND_STEPS
    has_prev = pl.program_id(1) > 0
    qi = jax.lax.broadcasted_iota(jnp.int32, (2 * s, 2 * s), 0) & (s - 1)
    kj = jax.lax.broadcasted_iota(jnp.int32, (2 * s, 2 * s), 1)
    band = (kj >= qi) & (kj <= qi + s)
    valid_first = band & ((kj >= s) | has_prev)
    first_head = jax.lax.broadcasted_iota(jnp.int32, (s, PAIR_LANES), 1) < HEAD_DIM

    for i in range(row_blocks):
        rows = slice(i * s, (i + 1) * s)
        for r in range(lane_blocks):
            for pair in range(N_ATT_HEADS // 2):
                lanes = slice(r * ATT_WIDTH + pair * PAIR_LANES,
                              r * ATT_WIDTH + (pair + 1) * PAIR_LANES)
                q2 = q_ref[rows, lanes]
                zero = jnp.zeros_like(q2)
                q = jnp.concatenate([jnp.where(first_head, q2, zero),
                                     jnp.where(first_head, zero, q2)], axis=0)
                if i == 0:
                    k_prev, v_prev, valid = kp_ref[:, lanes], vp_ref[:, lanes], valid_first
                else:
                    prev_rows = slice((i - 1) * s, i * s)
                    k_prev, v_prev, valid = kc_ref[prev_rows, lanes], vc_ref[prev_rows, lanes], band
                k = jnp.concatenate([k_prev, kc_ref[rows, lanes]], axis=0)
                v = jnp.concatenate([v_prev, vc_ref[rows, lanes]], axis=0)
                sc = jnp.where(valid, _dot_nt(q, k), NEG_INF)
                m = jnp.max(sc, axis=-1, keepdims=True)
                p = jnp.exp(sc - m)
                l = jnp.sum(p, axis=-1, keepdims=True)
                pv = _dot(p.astype(jnp.bfloat16), v) / l
                o_ref[rows, lanes] = jnp.where(first_head, pv[:s], pv[s:])
                lse = m + jnp.log(l)
                for half in range(2):
                    lo = r * LSE_LANES + (2 * pair + half) * LSE_REP
                    lse_ref[rows, lo:lo + LSE_REP] = jnp.broadcast_to(
                        lse[half * s:(half + 1) * s], (s, LSE_REP))


def _dilated_attention(q, k, v, dilation, row_blocks, lane_blocks):
    b, seq, _ = q.shape
    steps = seq // dilation
    tile_rows = row_blocks * BAND_STEPS
    assert steps % tile_rows == 0 and dilation % lane_blocks == 0
    view = lambda a: a.reshape(b, steps, dilation * ATT_WIDTH)
    width = lane_blocks * ATT_WIDTH
    cur = pl.BlockSpec((None, tile_rows, width), lambda bi, n, g: (bi, n, g))
    prev = pl.BlockSpec((None, BAND_STEPS, width),
                        lambda bi, n, g: (bi, jnp.maximum(n * row_blocks - 1, 0), g))
    o, lse = pl.pallas_call(
        functools.partial(_attn_kernel, row_blocks, lane_blocks),
        grid=(b, steps // tile_rows, dilation // lane_blocks),
        in_specs=[cur, cur, prev, cur, prev],
        out_specs=[cur, pl.BlockSpec((None, tile_rows, lane_blocks * LSE_LANES),
                                     lambda bi, n, g: (bi, n, g))],
        out_shape=[jax.ShapeDtypeStruct((b, steps, dilation * ATT_WIDTH), jnp.float32),
                   jax.ShapeDtypeStruct((b, steps, dilation * LSE_LANES), jnp.float32)],
        compiler_params=_params(3),
        name=f"dilated_attn_d{dilation}",
    )(view(q), view(k), view(k), view(v), view(v))
    return o.reshape(b * seq, ATT_WIDTH), lse.reshape(b * seq, LSE_LANES)


def _mix_out_kernel(o1_ref, o2_ref, o3_ref, l1_ref, l2_ref, l3_ref, c_ref, x_ref,
                    ga_ref, w_ref, y_ref):
    lses = [l1_ref[...], l2_ref[...], l3_ref[...]]
    top = jnp.maximum(jnp.maximum(lses[0], lses[1]), lses[2])
    es = [jnp.exp(l - top) for l in lses]
    den = es[0] + es[1] + es[2]
    ws = [e / den for e in es]
    heads = []
    for hd in range(N_ATT_HEADS):
        lanes = slice(hd * HEAD_DIM, (hd + 1) * HEAD_DIM)
        col = slice(hd * LSE_REP, hd * LSE_REP + 1)
        heads.append(ws[0][:, col] * o1_ref[:, lanes]
                     + ws[1][:, col] * o2_ref[:, lanes]
                     + ws[2][:, col] * o3_ref[:, lanes])
    attn = jnp.concatenate(heads, axis=-1)
    attn_n = _rms(attn, ga_ref[...]).astype(jnp.bfloat16)
    mixed = _dot(attn_n, w_ref[0:ATT_WIDTH, :]) + _dot(c_ref[...], w_ref[ATT_WIDTH:, :])
    y_ref[...] = x_ref[...] + mixed


def _mix_out(os, lses, convn, x2d, g_attn_out, w_out):
    t = x2d.shape[0]
    tm = TOKEN_TILE
    o_spec = pl.BlockSpec((tm, ATT_WIDTH), lambda i: (i, 0))
    l_spec = pl.BlockSpec((tm, LSE_LANES), lambda i: (i, 0))
    x_spec = pl.BlockSpec((tm, D_MODEL), lambda i: (i, 0))
    return pl.pallas_call(
        _mix_out_kernel,
        grid=(t // tm,),
        in_specs=[o_spec, o_spec, o_spec, l_spec, l_spec, l_spec, o_spec, x_spec,
                  _resident((1, ATT_WIDTH)), _resident(w_out.shape)],
        out_specs=x_spec,
        out_shape=jax.ShapeDtypeStruct((t, D_MODEL), jnp.float32),
        compiler_params=_params(1),
        name="mix_out_proj",
    )(*os, *lses, convn, x2d, g_attn_out.reshape(1, -1), w_out)


def _mem_kv_kernel(m_ref, g_ref, w_ref, kv_ref):
    h = _rms(m_ref[...], g_ref[...]).astype(jnp.bfloat16)
    kv_ref[...] = _dot(h, w_ref[...]).astype(jnp.bfloat16)


def _mem_kv(mem2d, g_mem, w_kv):
    rows = mem2d.shape[0]
    tm = TOKEN_TILE
    return pl.pallas_call(
        _mem_kv_kernel,
        grid=(rows // tm,),
        in_specs=[pl.BlockSpec((tm, D_MODEL), lambda i: (i, 0)),
                  _resident((1, D_MODEL)), _resident(w_kv.shape)],
        out_specs=pl.BlockSpec((tm, 2 * D_MODEL), lambda i: (i, 0)),
        out_shape=jax.ShapeDtypeStruct((rows, 2 * D_MODEL), jnp.bfloat16),
        compiler_params=_params(1),
        name="mem_kv_proj",
    )(mem2d, g_mem.reshape(1, -1), w_kv)


def _xattn_kernel(x_ref, g_ref, wq_ref, kv_ref, wo_ref, y_ref):
    x = x_ref[...]
    h = _rms(x, g_ref[...]).astype(jnp.bfloat16)
    heads = []
    for hd in range(N_MEM_HEADS):
        lanes = slice(hd * MEM_HEAD_DIM, (hd + 1) * MEM_HEAD_DIM)
        q = (_dot(h, wq_ref[:, lanes]) * (1.0 / math.sqrt(MEM_HEAD_DIM))).astype(jnp.bfloat16)
        k = kv_ref[:, lanes]
        v = kv_ref[:, D_MODEL + hd * MEM_HEAD_DIM:D_MODEL + (hd + 1) * MEM_HEAD_DIM]
        sc = _dot_nt(q, k)
        m = jnp.max(sc, axis=-1, keepdims=True)
        p = jnp.exp(sc - m)
        l = jnp.sum(p, axis=-1, keepdims=True)
        heads.append((_dot(p.astype(jnp.bfloat16), v) / l).astype(jnp.bfloat16))
    o = jnp.concatenate(heads, axis=-1)
    y_ref[...] = x + _dot(o, wo_ref[...])


def _xattn(x2d, g_xattn, w_q, kv, w_o, seq_len, mem_len):
    t = x2d.shape[0]
    tm = TOKEN_TILE
    tiles_per_seq = seq_len // tm
    x_spec = pl.BlockSpec((tm, D_MODEL), lambda i: (i, 0))
    return pl.pallas_call(
        _xattn_kernel,
        grid=(t // tm,),
        in_specs=[x_spec, _resident((1, D_MODEL)), _resident(w_q.shape),
                  pl.BlockSpec((mem_len, 2 * D_MODEL), lambda i: (i // tiles_per_seq, 0)),
                  _resident(w_o.shape)],
        out_specs=x_spec,
        out_shape=jax.ShapeDtypeStruct((t, D_MODEL), jnp.float32),
        compiler_params=_params(1),
        name="mem_xattn",
    )(x2d, g_xattn.reshape(1, -1), w_q, kv, w_o)


def _mlp_kernel(ff_chunk, x_ref, g_ref, wu_ref, wd_ref, gf_ref, y_ref):
    x = x_ref[...]
    h = _rms(x, g_ref[...]).astype(jnp.bfloat16)
    acc = x
    for c in range(D_FF // ff_chunk):
        cols = slice(c * ff_chunk, (c + 1) * ff_chunk)
        a = jnp.maximum(_dot(h, wu_ref[:, cols]), 0.0)
        acc = acc + _dot((a * a).astype(jnp.bfloat16), wd_ref[cols, :])
    y_ref[...] = _rms(acc, gf_ref[...])


def _mlp(x2d, g_mlp, w_up, w_down, g_final):
    t = x2d.shape[0]
    tm = TOKEN_TILE
    x_spec = pl.BlockSpec((tm, D_MODEL), lambda i: (i, 0))
    return pl.pallas_call(
        functools.partial(_mlp_kernel, 1024),
        grid=(t // tm,),
        in_specs=[x_spec, _resident((1, D_MODEL)), _resident(w_up.shape),
                  _resident(w_down.shape), _resident((1, D_MODEL))],
        out_specs=x_spec,
        out_shape=jax.ShapeDtypeStruct((t, D_MODEL), jnp.float32),
        compiler_params=_params(1),
        name="mlp_final_norm",
    )(x2d, g_mlp.reshape(1, -1), w_up, w_down, g_final.reshape(1, -1))


def kernel(x, mem, g_mix, w_in, conv_w, g_attn_out, g_conv_out, w_out, g_xattn, g_mem,
           w_q_mem, w_kv_mem, w_o_mem, g_mlp, w_up, w_down, g_final):
    b, seq, d = x.shape
    mem_len = mem.shape[1]
    bf = lambda w: w.astype(jnp.bfloat16)
    x2d = x.reshape(b * seq, d)

    q, k, v, convn = _in_proj(x2d, g_mix, bf(w_in), conv_w, g_conv_out, seq)
    q, k, v = (a.reshape(b, seq, ATT_WIDTH) for a in (q, k, v))
    os, lses = [], []
    for window, dilation in DILATED_PATTERNS:
        assert window // dilation == BAND_STEPS
        lane_blocks = min(dilation, ATTN_SUBTILES)
        o, lse = _dilated_attention(q, k, v, dilation, ATTN_SUBTILES // lane_blocks, lane_blocks)
        os.append(o)
        lses.append(lse)
    x1 = _mix_out(os, lses, convn, x2d, g_attn_out, bf(w_out))

    kv = _mem_kv(mem.reshape(b * mem_len, d), g_mem, bf(w_kv_mem))
    x2 = _xattn(x1, g_xattn, bf(w_q_mem), kv, bf(w_o_mem), seq, mem_len)
    y = _mlp(x2, g_mlp, bf(w_up), bf(w_down), g_final)
    return y.reshape(b, seq, d)
```

```python
import functools
import math

import jax
import jax.numpy as jnp
from jax.experimental import pallas as pl
from jax.experimental.pallas import tpu as pltpu

D_MODEL = 1024
ATT_WIDTH = 512
HEAD_DIM = 64
N_ATT_HEADS = 8
CONV_WIDTH = 512
CONV_K = 3
DILATED_PATTERNS = ((128, 1), (512, 4), (2048, 16))
BAND_STEPS = 128
N_MEM_HEADS = 4
MEM_HEAD_DIM = 256
D_FF = 4096
NORM_EPS = 1e-6
NEG_INF = -1e30

LANES = 128
F32_SUBLANES = 8
PAIR_LANES = 2 * HEAD_DIM
assert PAIR_LANES == LANES
N_SLABS = ATT_WIDTH // LANES
VMEM_LIMIT_BYTES = 56 * 1024 * 1024

TOKEN_TILE = 512
ATTN_SUBTILES = 4
RELAYOUT_RADIX = 4


def _rms(x, g):
    ms = jnp.mean(x * x, axis=-1, keepdims=True)
    return x * jax.lax.rsqrt(ms + NORM_EPS) * g


def _dot(a, b):
    return jnp.dot(a, b, preferred_element_type=jnp.float32)


def _dot_nt(a, b):
    return jax.lax.dot_general(a, b, (((1,), (1,)), ((), ())),
                               preferred_element_type=jnp.float32)


def _params(n_axes):
    return pltpu.CompilerParams(
        dimension_semantics=("arbitrary",) * n_axes,
        vmem_limit_bytes=VMEM_LIMIT_BYTES)


def _resident(shape):
    return pl.BlockSpec(shape, lambda *_: (0,) * len(shape))


def _slab(j):
    return slice(j * LANES, (j + 1) * LANES)


def _split_streams(val, out1_ref, out4_ref, out16_ref, nat_ref, by4_ref):
    tm = val.shape[0]
    n4, n16 = tm // 4, tm // 16
    out1_ref[...] = val.astype(jnp.bfloat16)
    for j in range(N_SLABS):
        nat_ref[j] = val[:, _slab(j)]
    for r in range(RELAYOUT_RADIX):
        for j in range(N_SLABS):
            piece = nat_ref[j, pl.ds(r, n4, stride=RELAYOUT_RADIX), :]
            by4_ref[j, r * n4:(r + 1) * n4, :] = piece
            out4_ref[r, :, _slab(j)] = piece.astype(jnp.bfloat16)
    for r4 in range(RELAYOUT_RADIX):
        for a in range(RELAYOUT_RADIX):
            for j in range(N_SLABS):
                piece = by4_ref[j, pl.ds(r4 * n4 + a, n16, stride=RELAYOUT_RADIX), :]
                out16_ref[RELAYOUT_RADIX * a + r4, :, _slab(j)] = piece.astype(jnp.bfloat16)


def _merge_streams4(in4_ref, nat_ref):
    n4 = in4_ref.shape[1]
    for r in range(RELAYOUT_RADIX):
        for j in range(N_SLABS):
            nat_ref[j, pl.ds(r, n4, stride=RELAYOUT_RADIX), :] = in4_ref[r, :, _slab(j)]


def _merge_streams16(in16_ref, by4_ref, nat_ref):
    n16 = in16_ref.shape[1]
    n4 = RELAYOUT_RADIX * n16
    for r4 in range(RELAYOUT_RADIX):
        for a in range(RELAYOUT_RADIX):
            for j in range(N_SLABS):
                by4_ref[j, pl.ds(r4 * n4 + a, n16, stride=RELAYOUT_RADIX), :] = (
                    in16_ref[RELAYOUT_RADIX * a + r4, :, _slab(j)])
    for r4 in range(RELAYOUT_RADIX):
        for j in range(N_SLABS):
            nat_ref[j, pl.ds(r4, n4, stride=RELAYOUT_RADIX), :] = (
                by4_ref[j, r4 * n4:(r4 + 1) * n4, :])


def _in_proj_kernel(tiles_per_seq, x_ref, g_ref, w_ref, cw_ref, gc_ref,
                    q1_ref, k1_ref, v1_ref, q4_ref, k4_ref, v4_ref,
                    q16_ref, k16_ref, v16_ref, c_ref, ubuf_ref, nat_ref, by4_ref):
    tm = x_ref.shape[0]
    h = _rms(x_ref[...], g_ref[...]).astype(jnp.bfloat16)

    def proj(col):
        return _dot(h, w_ref[:, col * ATT_WIDTH:(col + 1) * ATT_WIDTH])

    _split_streams(proj(0) * (1.0 / math.sqrt(HEAD_DIM)), q1_ref, q4_ref, q16_ref,
                   nat_ref, by4_ref)
    _split_streams(proj(1), k1_ref, k4_ref, k16_ref, nat_ref, by4_ref)
    _split_streams(proj(2), v1_ref, v4_ref, v16_ref, nat_ref, by4_ref)

    @pl.when(pl.program_id(0) % tiles_per_seq == 0)
    def _():
        ubuf_ref[tm:tm + F32_SUBLANES, :] = jnp.zeros((F32_SUBLANES, CONV_WIDTH), jnp.float32)

    ubuf_ref[0:F32_SUBLANES, :] = ubuf_ref[tm:tm + F32_SUBLANES, :]
    u = proj(4) * proj(5)
    ubuf_ref[F32_SUBLANES:tm + F32_SUBLANES, :] = u
    conv = (u * cw_ref[2:3, :]
            + ubuf_ref[F32_SUBLANES - 1:tm + F32_SUBLANES - 1, :] * cw_ref[1:2, :]
            + ubuf_ref[F32_SUBLANES - 2:tm + F32_SUBLANES - 2, :] * cw_ref[0:1, :])
    y = proj(3) * conv
    c_ref[...] = _rms(y, gc_ref[...]).astype(jnp.bfloat16)


def _stream_spec(dilation, tm, tiles_per_seq):
    return pl.BlockSpec((None, dilation, tm // dilation, ATT_WIDTH),
                        lambda i: (i // tiles_per_seq, 0, i % tiles_per_seq, 0))


def _in_proj(x2d, g_mix, w_in, conv_w, g_conv_out, batch, seq_len):
    t = x2d.shape[0]
    tm = TOKEN_TILE
    assert seq_len % tm == 0 and t == batch * seq_len
    tiles_per_seq = seq_len // tm
    nat = jax.ShapeDtypeStruct((t, ATT_WIDTH), jnp.bfloat16)
    by = lambda d: jax.ShapeDtypeStruct((batch, d, seq_len // d, ATT_WIDTH), jnp.bfloat16)
    row_spec = pl.BlockSpec((tm, ATT_WIDTH), lambda i: (i, 0))
    s4, s16 = _stream_spec(4, tm, tiles_per_seq), _stream_spec(16, tm, tiles_per_seq)
    return pl.pallas_call(
        functools.partial(_in_proj_kernel, tiles_per_seq),
        grid=(t // tm,),
        in_specs=[
            pl.BlockSpec((tm, D_MODEL), lambda i: (i, 0)),
            _resident((1, D_MODEL)),
            _resident(w_in.shape),
            _resident(conv_w.shape),
            _resident((1, CONV_WIDTH)),
        ],
        out_specs=[row_spec] * 3 + [s4] * 3 + [s16] * 3 + [row_spec],
        out_shape=[nat] * 3 + [by(4)] * 3 + [by(16)] * 3 + [nat],
        scratch_shapes=[pltpu.VMEM((tm + F32_SUBLANES, CONV_WIDTH), jnp.float32),
                        pltpu.VMEM((N_SLABS, tm, LANES), jnp.float32),
                        pltpu.VMEM((N_SLABS, tm, LANES), jnp.float32)],
        compiler_params=_params(1),
        name="in_proj",
    )(x2d, g_mix.reshape(1, -1), w_in, conv_w, g_conv_out.reshape(1, -1))


def _attn_kernel(q_ref, kc_ref, kp_ref, vc_ref, vp_ref, o_ref, lse_ref):
    s = BAND_STEPS
    streams, row_blocks = q_ref.shape[0], q_ref.shape[1] // s
    has_prev = pl.program_id(2) > 0
    qi = jax.lax.broadcasted_iota(jnp.int32, (2 * s, 2 * s), 0) & (s - 1)
    kj = jax.lax.broadcasted_iota(jnp.int32, (2 * s, 2 * s), 1)
    band = (kj >= qi) & (kj <= qi + s)
    valid_first = band & ((kj >= s) | has_prev)
    first_head = jax.lax.broadcasted_iota(jnp.int32, (s, PAIR_LANES), 1) < HEAD_DIM

    for r in range(streams):
        for i in range(row_blocks):
            rows = slice(i * s, (i + 1) * s)
            for pair in range(N_ATT_HEADS // 2):
                lanes = _slab(pair)
                q2 = q_ref[r, rows, lanes]
                zero = jnp.zeros_like(q2)
                q = jnp.concatenate([jnp.where(first_head, q2, zero),
                                     jnp.where(first_head, zero, q2)], axis=0)
                if i == 0:
                    k_prev, v_prev, valid = kp_ref[r, :, lanes], vp_ref[r, :, lanes], valid_first
                else:
                    prev_rows = slice((i - 1) * s, i * s)
                    k_prev, v_prev = kc_ref[r, prev_rows, lanes], vc_ref[r, prev_rows, lanes]
                    valid = band
                k = jnp.concatenate([k_prev, kc_ref[r, rows, lanes]], axis=0)
                v = jnp.concatenate([v_prev, vc_ref[r, rows, lanes]], axis=0)
                sc = jnp.where(valid, _dot_nt(q, k), NEG_INF)
                m = jnp.max(sc, axis=-1, keepdims=True)
                p = jnp.exp(sc - m)
                l = jnp.sum(p, axis=-1, keepdims=True)
                pv = _dot(p.astype(jnp.bfloat16), v) / l
                o_ref[r, rows, lanes] = jnp.where(first_head, pv[:s], pv[s:])
                lse = jnp.broadcast_to(m + jnp.log(l), (2 * s, PAIR_LANES))
                lse_ref[r, rows, lanes] = jnp.where(first_head, lse[:s], lse[s:])


def _dilated_attention(q, k, v):
    b, d, steps, _ = q.shape
    row_blocks = min(ATTN_SUBTILES, steps // BAND_STEPS)
    streams = ATTN_SUBTILES // row_blocks
    tile_rows = row_blocks * BAND_STEPS
    assert steps % tile_rows == 0 and d % streams == 0
    cur = pl.BlockSpec((None, streams, tile_rows, ATT_WIDTH), lambda bi, g, n: (bi, g, n, 0))
    prev = pl.BlockSpec((None, streams, BAND_STEPS, ATT_WIDTH),
                        lambda bi, g, n: (bi, g, jnp.maximum(n * row_blocks - 1, 0), 0))
    out = jax.ShapeDtypeStruct(q.shape, jnp.float32)
    return pl.pallas_call(
        _attn_kernel,
        grid=(b, d // streams, steps // tile_rows),
        in_specs=[cur, cur, prev, cur, prev],
        out_specs=[cur, cur],
        out_shape=[out, out],
        compiler_params=_params(3),
        name=f"dilated_attn_d{d}",
    )(q, k, k, v, v)


def _mix_out_kernel(o1_ref, l1_ref, o4_ref, l4_ref, o16_ref, l16_ref, c_ref, x_ref,
                    ga_ref, w_ref, y_ref, on4_ref, ln4_ref, on16_ref, ln16_ref, by4_ref):
    _merge_streams4(o4_ref, on4_ref)
    _merge_streams4(l4_ref, ln4_ref)
    _merge_streams16(o16_ref, by4_ref, on16_ref)
    _merge_streams16(l16_ref, by4_ref, ln16_ref)
    slabs = []
    for j in range(N_SLABS):
        lses = [l1_ref[:, _slab(j)], ln4_ref[j], ln16_ref[j]]
        outs = [o1_ref[:, _slab(j)], on4_ref[j], on16_ref[j]]
        top = jnp.maximum(jnp.maximum(lses[0], lses[1]), lses[2])
        es = [jnp.exp(l - top) for l in lses]
        num = es[0] * outs[0] + es[1] * outs[1] + es[2] * outs[2]
        slabs.append(num / (es[0] + es[1] + es[2]))
    attn = jnp.concatenate(slabs, axis=-1)
    attn_n = _rms(attn, ga_ref[...]).astype(jnp.bfloat16)
    mixed = _dot(attn_n, w_ref[0:ATT_WIDTH, :]) + _dot(c_ref[...], w_ref[ATT_WIDTH:, :])
    y_ref[...] = x_ref[...] + mixed


def _mix_out(o1, l1, o4, l4, o16, l16, convn, x2d, g_attn_out, w_out, seq_len):
    t = x2d.shape[0]
    tm = TOKEN_TILE
    tiles_per_seq = seq_len // tm
    row_spec = pl.BlockSpec((tm, ATT_WIDTH), lambda i: (i, 0))
    s4, s16 = _stream_spec(4, tm, tiles_per_seq), _stream_spec(16, tm, tiles_per_seq)
    x_spec = pl.BlockSpec((tm, D_MODEL), lambda i: (i, 0))
    nat = pltpu.VMEM((N_SLABS, tm, LANES), jnp.float32)
    return pl.pallas_call(
        _mix_out_kernel,
        grid=(t // tm,),
        in_specs=[row_spec, row_spec, s4, s4, s16, s16, row_spec, x_spec,
                  _resident((1, ATT_WIDTH)), _resident(w_out.shape)],
        out_specs=x_spec,
        out_shape=jax.ShapeDtypeStruct((t, D_MODEL), jnp.float32),
        scratch_shapes=[nat] * 5,
        compiler_params=_params(1),
        name="mix_out_proj",
    )(o1, l1, o4, l4, o16, l16, convn, x2d, g_attn_out.reshape(1, -1), w_out)


def _mem_kv_kernel(m_ref, g_ref, w_ref, kv_ref):
    h = _rms(m_ref[...], g_ref[...]).astype(jnp.bfloat16)
    kv_ref[...] = _dot(h, w_ref[...]).astype(jnp.bfloat16)


def _mem_kv(mem2d, g_mem, w_kv):
    rows = mem2d.shape[0]
    tm = TOKEN_TILE
    return pl.pallas_call(
        _mem_kv_kernel,
        grid=(rows // tm,),
        in_specs=[pl.BlockSpec((tm, D_MODEL), lambda i: (i, 0)),
                  _resident((1, D_MODEL)), _resident(w_kv.shape)],
        out_specs=pl.BlockSpec((tm, 2 * D_MODEL), lambda i: (i, 0)),
        out_shape=jax.ShapeDtypeStruct((rows, 2 * D_MODEL), jnp.bfloat16),
        compiler_params=_params(1),
        name="mem_kv_proj",
    )(mem2d, g_mem.reshape(1, -1), w_kv)


def _xattn_kernel(x_ref, g_ref, wq_ref, kv_ref, wo_ref, y_ref):
    x = x_ref[...]
    h = _rms(x, g_ref[...]).astype(jnp.bfloat16)
    heads = []
    for hd in range(N_MEM_HEADS):
        lanes = slice(hd * MEM_HEAD_DIM, (hd + 1) * MEM_HEAD_DIM)
        q = (_dot(h, wq_ref[:, lanes]) * (1.0 / math.sqrt(MEM_HEAD_DIM))).astype(jnp.bfloat16)
        k = kv_ref[:, lanes]
        v = kv_ref[:, D_MODEL + hd * MEM_HEAD_DIM:D_MODEL + (hd + 1) * MEM_HEAD_DIM]
        sc = _dot_nt(q, k)
        m = jnp.max(sc, axis=-1, keepdims=True)
        p = jnp.exp(sc - m)
        l = jnp.sum(p, axis=-1, keepdims=True)
        heads.append((_dot(p.astype(jnp.bfloat16), v) / l).astype(jnp.bfloat16))
    o = jnp.concatenate(heads, axis=-1)
    y_ref[...] = x + _dot(o, wo_ref[...])


def _xattn(x2d, g_xattn, w_q, kv, w_o, seq_len, mem_len):
    t = x2d.shape[0]
    tm = TOKEN_TILE
    tiles_per_seq = seq_len // tm
    x_spec = pl.BlockSpec((tm, D_MODEL), lambda i: (i, 0))
    return pl.pallas_call(
        _xattn_kernel,
        grid=(t // tm,),
        in_specs=[x_spec, _resident((1, D_MODEL)), _resident(w_q.shape),
                  pl.BlockSpec((mem_len, 2 * D_MODEL), lambda i: (i // tiles_per_seq, 0)),
                  _resident(w_o.shape)],
        out_specs=x_spec,
        out_shape=jax.ShapeDtypeStruct((t, D_MODEL), jnp.float32),
        compiler_params=_params(1),
        name="mem_xattn",
    )(x2d, g_xattn.reshape(1, -1), w_q, kv, w_o)


def _mlp_kernel(ff_chunk, x_ref, g_ref, wu_ref, wd_ref, gf_ref, y_ref):
    x = x_ref[...]
    h = _rms(x, g_ref[...]).astype(jnp.bfloat16)
    acc = x
    for c in range(D_FF // ff_chunk):
        cols = slice(c * ff_chunk, (c + 1) * ff_chunk)
        a = jnp.maximum(_dot(h, wu_ref[:, cols]), 0.0)
        acc = acc + _dot((a * a).astype(jnp.bfloat16), wd_ref[cols, :])
    y_ref[...] = _rms(acc, gf_ref[...])


def _mlp(x2d, g_mlp, w_up, w_down, g_final):
    t = x2d.shape[0]
    tm = TOKEN_TILE
    x_spec = pl.BlockSpec((tm, D_MODEL), lambda i: (i, 0))
    return pl.pallas_call(
        functools.partial(_mlp_kernel, 1024),
        grid=(t // tm,),
        in_specs=[x_spec, _resident((1, D_MODEL)), _resident(w_up.shape),
                  _resident(w_down.shape), _resident((1, D_MODEL))],
        out_specs=x_spec,
        out_shape=jax.ShapeDtypeStruct((t, D_MODEL), jnp.float32),
        compiler_params=_params(1),
        name="mlp_final_norm",
    )(x2d, g_mlp.reshape(1, -1), w_up, w_down, g_final.reshape(1, -1))


def kernel(x, mem, g_mix, w_in, conv_w, g_attn_out, g_conv_out, w_out, g_xattn, g_mem,
           w_q_mem, w_kv_mem, w_o_mem, g_mlp, w_up, w_down, g_final):
    b, seq, d = x.shape
    mem_len = mem.shape[1]
    assert tuple(dil for _, dil in DILATED_PATTERNS) == (1, 4, 16)
    assert all(win // dil == BAND_STEPS for win, dil in DILATED_PATTERNS)
    bf = lambda w: w.astype(jnp.bfloat16)
    x2d = x.reshape(b * seq, d)

    (q1, k1, v1, q4, k4, v4, q16, k16, v16, convn) = _in_proj(
        x2d, g_mix, bf(w_in), conv_w, g_conv_out, b, seq)
    as_stream = lambda a: a.reshape(b, 1, seq, ATT_WIDTH)
    o1, l1 = _dilated_attention(as_stream(q1), as_stream(k1), as_stream(v1))
    o4, l4 = _dilated_attention(q4, k4, v4)
    o16, l16 = _dilated_attention(q16, k16, v16)
    flat = lambda a: a.reshape(b * seq, ATT_WIDTH)
    x1 = _mix_out(flat(o1), flat(l1), o4, l4, o16, l16, convn, x2d, g_attn_out, bf(w_out), seq)

    kv = _mem_kv(mem.reshape(b * mem_len, d), g_mem, bf(w_kv_mem))
    x2 = _xattn(x1, g_xattn, bf(w_q_mem), kv, bf(w_o_mem), seq, mem_len)
    y = _mlp(x2, g_mlp, bf(w_up), bf(w_down), g_final)
    return y.reshape(b, seq, d)
```

```python
import functools
import math

import jax
import jax.numpy as jnp
from jax.experimental import pallas as pl
from jax.experimental.pallas import tpu as pltpu

D_MODEL = 1024
ATT_WIDTH = 512
HEAD_DIM = 64
N_ATT_HEADS = 8
CONV_WIDTH = 512
CONV_K = 3
DILATED_PATTERNS = ((128, 1), (512, 4), (2048, 16))
BAND_STEPS = 128
N_MEM_HEADS = 4
MEM_HEAD_DIM = 256
D_FF = 4096
NORM_EPS = 1e-6
NEG_INF = -1e30

LANES = 128
F32_SUBLANES = 8
PAIR_LANES = 2 * HEAD_DIM
assert PAIR_LANES == LANES
N_SLABS = ATT_WIDTH // LANES
VMEM_LIMIT_BYTES = 56 * 1024 * 1024

TOKEN_TILE = 512
POST_TILE = 256
FF_CHUNK = 1024
ATTN_SUBTILES = 4
RELAYOUT_RADIX = 4


def _rms(x, g):
    ms = jnp.mean(x * x, axis=-1, keepdims=True)
    return x * jax.lax.rsqrt(ms + NORM_EPS) * g


def _dot(a, b):
    return jnp.dot(a, b, preferred_element_type=jnp.float32)


def _dot_nt(a, b):
    return jax.lax.dot_general(a, b, (((1,), (1,)), ((), ())),
                               preferred_element_type=jnp.float32)


def _params(n_axes):
    return pltpu.CompilerParams(
        dimension_semantics=("arbitrary",) * n_axes,
        vmem_limit_bytes=VMEM_LIMIT_BYTES)


def _resident(shape):
    return pl.BlockSpec(shape, lambda *_: (0,) * len(shape), pipeline_mode=pl.Buffered(1))


def _slab(j):
    return slice(j * LANES, (j + 1) * LANES)


def _split_streams(val, out1_ref, out4_ref, out16_ref, nat_ref, by4_ref):
    tm = val.shape[0]
    n4, n16 = tm // 4, tm // 16
    out1_ref[...] = val.astype(jnp.bfloat16)
    for j in range(N_SLABS):
        nat_ref[j] = val[:, _slab(j)]
    for r in range(RELAYOUT_RADIX):
        for j in range(N_SLABS):
            piece = nat_ref[j, pl.ds(r, n4, stride=RELAYOUT_RADIX), :]
            by4_ref[j, r * n4:(r + 1) * n4, :] = piece
            out4_ref[r, :, _slab(j)] = piece.astype(jnp.bfloat16)
    for r4 in range(RELAYOUT_RADIX):
        for a in range(RELAYOUT_RADIX):
            for j in range(N_SLABS):
                piece = by4_ref[j, pl.ds(r4 * n4 + a, n16, stride=RELAYOUT_RADIX), :]
                out16_ref[RELAYOUT_RADIX * a + r4, :, _slab(j)] = piece.astype(jnp.bfloat16)


def _merge_streams4(in4_ref, nat_ref):
    n4 = in4_ref.shape[1]
    for r in range(RELAYOUT_RADIX):
        for j in range(N_SLABS):
            nat_ref[j, pl.ds(r, n4, stride=RELAYOUT_RADIX), :] = in4_ref[r, :, _slab(j)]


def _merge_streams16(in16_ref, by4_ref, nat_ref):
    n16 = in16_ref.shape[1]
    n4 = RELAYOUT_RADIX * n16
    for r4 in range(RELAYOUT_RADIX):
        for a in range(RELAYOUT_RADIX):
            for j in range(N_SLABS):
                by4_ref[j, pl.ds(r4 * n4 + a, n16, stride=RELAYOUT_RADIX), :] = (
                    in16_ref[RELAYOUT_RADIX * a + r4, :, _slab(j)])
    for r4 in range(RELAYOUT_RADIX):
        for j in range(N_SLABS):
            nat_ref[j, pl.ds(r4, n4, stride=RELAYOUT_RADIX), :] = (
                by4_ref[j, r4 * n4:(r4 + 1) * n4, :])


def _in_proj_kernel(tiles_per_seq, x_ref, g_ref, w_ref, cw_ref, gc_ref,
                    q1_ref, k1_ref, v1_ref, q4_ref, k4_ref, v4_ref,
                    q16_ref, k16_ref, v16_ref, c_ref, ubuf_ref, nat_ref, by4_ref):
    tm = x_ref.shape[0]
    h = _rms(x_ref[...], g_ref[...]).astype(jnp.bfloat16)

    def proj(col):
        return _dot(h, w_ref[:, col * ATT_WIDTH:(col + 1) * ATT_WIDTH])

    _split_streams(proj(0) * (1.0 / math.sqrt(HEAD_DIM)), q1_ref, q4_ref, q16_ref,
                   nat_ref, by4_ref)
    _split_streams(proj(1), k1_ref, k4_ref, k16_ref, nat_ref, by4_ref)
    _split_streams(proj(2), v1_ref, v4_ref, v16_ref, nat_ref, by4_ref)

    @pl.when(pl.program_id(0) % tiles_per_seq == 0)
    def _():
        ubuf_ref[tm:tm + F32_SUBLANES, :] = jnp.zeros((F32_SUBLANES, CONV_WIDTH), jnp.float32)

    ubuf_ref[0:F32_SUBLANES, :] = ubuf_ref[tm:tm + F32_SUBLANES, :]
    u = proj(4) * proj(5)
    ubuf_ref[F32_SUBLANES:tm + F32_SUBLANES, :] = u
    conv = (u * cw_ref[2:3, :]
            + ubuf_ref[F32_SUBLANES - 1:tm + F32_SUBLANES - 1, :] * cw_ref[1:2, :]
            + ubuf_ref[F32_SUBLANES - 2:tm + F32_SUBLANES - 2, :] * cw_ref[0:1, :])
    y = proj(3) * conv
    c_ref[...] = _rms(y, gc_ref[...]).astype(jnp.bfloat16)


def _stream_spec(dilation, tm, tiles_per_seq):
    return pl.BlockSpec((None, dilation, tm // dilation, ATT_WIDTH),
                        lambda i: (i // tiles_per_seq, 0, i % tiles_per_seq, 0))


def _in_proj(x2d, g_mix, w_in, conv_w, g_conv_out, batch, seq_len):
    t = x2d.shape[0]
    tm = TOKEN_TILE
    assert seq_len % tm == 0 and t == batch * seq_len
    tiles_per_seq = seq_len // tm
    nat = jax.ShapeDtypeStruct((t, ATT_WIDTH), jnp.bfloat16)
    by = lambda d: jax.ShapeDtypeStruct((batch, d, seq_len // d, ATT_WIDTH), jnp.bfloat16)
    row_spec = pl.BlockSpec((tm, ATT_WIDTH), lambda i: (i, 0))
    s4, s16 = _stream_spec(4, tm, tiles_per_seq), _stream_spec(16, tm, tiles_per_seq)
    return pl.pallas_call(
        functools.partial(_in_proj_kernel, tiles_per_seq),
        grid=(t // tm,),
        in_specs=[
            pl.BlockSpec((tm, D_MODEL), lambda i: (i, 0)),
            _resident((1, D_MODEL)),
            _resident(w_in.shape),
            _resident(conv_w.shape),
            _resident((1, CONV_WIDTH)),
        ],
        out_specs=[row_spec] * 3 + [s4] * 3 + [s16] * 3 + [row_spec],
        out_shape=[nat] * 3 + [by(4)] * 3 + [by(16)] * 3 + [nat],
        scratch_shapes=[pltpu.VMEM((tm + F32_SUBLANES, CONV_WIDTH), jnp.float32),
                        pltpu.VMEM((N_SLABS, tm, LANES), jnp.float32),
                        pltpu.VMEM((N_SLABS, tm, LANES), jnp.float32)],
        compiler_params=_params(1),
        name="in_proj",
    )(x2d, g_mix.reshape(1, -1), w_in, conv_w, g_conv_out.reshape(1, -1))


def _attn_kernel(q_ref, kc_ref, kp_ref, vc_ref, vp_ref, o_ref, lse_ref):
    s = BAND_STEPS
    streams, row_blocks = q_ref.shape[0], q_ref.shape[1] // s
    has_prev = pl.program_id(2) > 0
    qi = jax.lax.broadcasted_iota(jnp.int32, (2 * s, 2 * s), 0) & (s - 1)
    kj = jax.lax.broadcasted_iota(jnp.int32, (2 * s, 2 * s), 1)
    band = (kj >= qi) & (kj <= qi + s)
    valid_first = band & ((kj >= s) | has_prev)
    first_head = jax.lax.broadcasted_iota(jnp.int32, (s, PAIR_LANES), 1) < HEAD_DIM

    for r in range(streams):
        for i in range(row_blocks):
            rows = slice(i * s, (i + 1) * s)
            for pair in range(N_ATT_HEADS // 2):
                lanes = _slab(pair)
                q2 = q_ref[r, rows, lanes]
                zero = jnp.zeros_like(q2)
                q = jnp.concatenate([jnp.where(first_head, q2, zero),
                                     jnp.where(first_head, zero, q2)], axis=0)
                if i == 0:
                    k_prev, v_prev, valid = kp_ref[r, :, lanes], vp_ref[r, :, lanes], valid_first
                else:
                    prev_rows = slice((i - 1) * s, i * s)
                    k_prev, v_prev = kc_ref[r, prev_rows, lanes], vc_ref[r, prev_rows, lanes]
                    valid = band
                k = jnp.concatenate([k_prev, kc_ref[r, rows, lanes]], axis=0)
                v = jnp.concatenate([v_prev, vc_ref[r, rows, lanes]], axis=0)
                sc = jnp.where(valid, _dot_nt(q, k), NEG_INF)
                m = jnp.max(sc, axis=-1, keepdims=True)
                p = jnp.exp(sc - m)
                l = jnp.sum(p, axis=-1, keepdims=True)
                pv = _dot(p.astype(jnp.bfloat16), v) / l
                o_ref[r, rows, lanes] = jnp.where(first_head, pv[:s], pv[s:])
                lse = jnp.broadcast_to(m + jnp.log(l), (2 * s, PAIR_LANES))
                lse_ref[r, rows, lanes] = jnp.where(first_head, lse[:s], lse[s:])


def _dilated_attention(q, k, v):
    b, d, steps, _ = q.shape
    row_blocks = min(ATTN_SUBTILES, steps // BAND_STEPS)
    streams = ATTN_SUBTILES // row_blocks
    tile_rows = row_blocks * BAND_STEPS
    assert steps % tile_rows == 0 and d % streams == 0
    cur = pl.BlockSpec((None, streams, tile_rows, ATT_WIDTH), lambda bi, g, n: (bi, g, n, 0))
    prev = pl.BlockSpec((None, streams, BAND_STEPS, ATT_WIDTH),
                        lambda bi, g, n: (bi, g, jnp.maximum(n * row_blocks - 1, 0), 0))
    out = jax.ShapeDtypeStruct(q.shape, jnp.float32)
    return pl.pallas_call(
        _attn_kernel,
        grid=(b, d // streams, steps // tile_rows),
        in_specs=[cur, cur, prev, cur, prev],
        out_specs=[cur, cur],
        out_shape=[out, out],
        compiler_params=_params(3),
        name=f"dilated_attn_d{d}",
    )(q, k, k, v, v)


def _post_attn_kernel(o1_ref, l1_ref, o4_ref, l4_ref, o16_ref, l16_ref, c_ref, x_ref,
                      ga_ref, wout_ref, gx_ref, wq_ref, kv_ref, wo_ref,
                      gm_ref, wu_ref, wd_ref, gf_ref, y_ref,
                      on4_ref, ln4_ref, on16_ref, ln16_ref, by4_ref):
    _merge_streams4(o4_ref, on4_ref)
    _merge_streams4(l4_ref, ln4_ref)
    _merge_streams16(o16_ref, by4_ref, on16_ref)
    _merge_streams16(l16_ref, by4_ref, ln16_ref)
    slabs = []
    for j in range(N_SLABS):
        lses = [l1_ref[:, _slab(j)], ln4_ref[j], ln16_ref[j]]
        outs = [o1_ref[:, _slab(j)], on4_ref[j], on16_ref[j]]
        top = jnp.maximum(jnp.maximum(lses[0], lses[1]), lses[2])
        es = [jnp.exp(l - top) for l in lses]
        num = es[0] * outs[0] + es[1] * outs[1] + es[2] * outs[2]
        slabs.append(num / (es[0] + es[1] + es[2]))
    attn = jnp.concatenate(slabs, axis=-1)
    attn_n = _rms(attn, ga_ref[...]).astype(jnp.bfloat16)
    x = (x_ref[...] + _dot(attn_n, wout_ref[0:ATT_WIDTH, :])
         + _dot(c_ref[...], wout_ref[ATT_WIDTH:, :]))

    h = _rms(x, gx_ref[...]).astype(jnp.bfloat16)
    heads = []
    for hd in range(N_MEM_HEADS):
        lanes = slice(hd * MEM_HEAD_DIM, (hd + 1) * MEM_HEAD_DIM)
        q = (_dot(h, wq_ref[:, lanes]) * (1.0 / math.sqrt(MEM_HEAD_DIM))).astype(jnp.bfloat16)
        k = kv_ref[:, lanes]
        v = kv_ref[:, D_MODEL + hd * MEM_HEAD_DIM:D_MODEL + (hd + 1) * MEM_HEAD_DIM]
        sc = _dot_nt(q, k)
        m = jnp.max(sc, axis=-1, keepdims=True)
        p = jnp.exp(sc - m)
        l = jnp.sum(p, axis=-1, keepdims=True)
        heads.append((_dot(p.astype(jnp.bfloat16), v) / l).astype(jnp.bfloat16))
    x = x + _dot(jnp.concatenate(heads, axis=-1), wo_ref[...])

    h = _rms(x, gm_ref[...]).astype(jnp.bfloat16)
    for c in range(D_FF // FF_CHUNK):
        cols = slice(c * FF_CHUNK, (c + 1) * FF_CHUNK)
        a = jnp.maximum(_dot(h, wu_ref[:, cols]), 0.0)
        x = x + _dot((a * a).astype(jnp.bfloat16), wd_ref[cols, :])
    y_ref[...] = _rms(x, gf_ref[...])


def _post_attn(o1, l1, o4, l4, o16, l16, convn, x2d, kv, g_attn_out, w_out, g_xattn, w_q, w_o,
               g_mlp, w_up, w_down, g_final, seq_len, mem_len):
    t = x2d.shape[0]
    tm = POST_TILE
    assert seq_len % tm == 0
    tiles_per_seq = seq_len // tm
    row_spec = pl.BlockSpec((tm, ATT_WIDTH), lambda i: (i, 0))
    s4, s16 = _stream_spec(4, tm, tiles_per_seq), _stream_spec(16, tm, tiles_per_seq)
    x_spec = pl.BlockSpec((tm, D_MODEL), lambda i: (i, 0))
    gain = lambda g: g.reshape(1, -1)
    nat = pltpu.VMEM((N_SLABS, tm, LANES), jnp.float32)
    return pl.pallas_call(
        _post_attn_kernel,
        grid=(t // tm,),
        in_specs=[row_spec, row_spec, s4, s4, s16, s16, row_spec, x_spec,
                  _resident((1, ATT_WIDTH)), _resident(w_out.shape),
                  _resident((1, D_MODEL)), _resident(w_q.shape),
                  pl.BlockSpec((mem_len, 2 * D_MODEL), lambda i: (i // tiles_per_seq, 0)),
                  _resident(w_o.shape),
                  _resident((1, D_MODEL)), _resident(w_up.shape), _resident(w_down.shape),
                  _resident((1, D_MODEL))],
        out_specs=x_spec,
        out_shape=jax.ShapeDtypeStruct((t, D_MODEL), jnp.float32),
        scratch_shapes=[nat] * 5,
        compiler_params=_params(1),
        name="post_attn",
    )(o1, l1, o4, l4, o16, l16, convn, x2d, gain(g_attn_out), w_out, gain(g_xattn), w_q, kv, w_o,
      gain(g_mlp), w_up, w_down, gain(g_final))


def _mem_kv_kernel(m_ref, g_ref, w_ref, kv_ref):
    h = _rms(m_ref[...], g_ref[...]).astype(jnp.bfloat16)
    kv_ref[...] = _dot(h, w_ref[...]).astype(jnp.bfloat16)


def _mem_kv(mem2d, g_mem, w_kv):
    rows = mem2d.shape[0]
    tm = TOKEN_TILE
    return pl.pallas_call(
        _mem_kv_kernel,
        grid=(rows // tm,),
        in_specs=[pl.BlockSpec((tm, D_MODEL), lambda i: (i, 0)),
                  _resident((1, D_MODEL)), _resident(w_kv.shape)],
        out_specs=pl.BlockSpec((tm, 2 * D_MODEL), lambda i: (i, 0)),
        out_shape=jax.ShapeDtypeStruct((rows, 2 * D_MODEL), jnp.bfloat16),
        compiler_params=_params(1),
        name="mem_kv_proj",
    )(mem2d, g_mem.reshape(1, -1), w_kv)


def kernel(x, mem, g_mix, w_in, conv_w, g_attn_out, g_conv_out, w_out, g_xattn, g_mem,
           w_q_mem, w_kv_mem, w_o_mem, g_mlp, w_up, w_down, g_final):
    b, seq, d = x.shape
    mem_len = mem.shape[1]
    assert tuple(dil for _, dil in DILATED_PATTERNS) == (1, 4, 16)
    assert all(win // dil == BAND_STEPS for win, dil in DILATED_PATTERNS)
    bf = lambda w: w.astype(jnp.bfloat16)
    x2d = x.reshape(b * seq, d)

    (q1, k1, v1, q4, k4, v4, q16, k16, v16, convn) = _in_proj(
        x2d, g_mix, bf(w_in), conv_w, g_conv_out, b, seq)
    as_stream = lambda a: a.reshape(b, 1, seq, ATT_WIDTH)
    o1, l1 = _dilated_attention(as_stream(q1), as_stream(k1), as_stream(v1))
    o4, l4 = _dilated_attention(q4, k4, v4)
    o16, l16 = _dilated_attention(q16, k16, v16)
    flat = lambda a: a.reshape(b * seq, ATT_WIDTH)
    kv = _mem_kv(mem.reshape(b * mem_len, d), g_mem, bf(w_kv_mem))
    y = _post_attn(flat(o1), flat(l1), o4, l4, o16, l16, convn, x2d, kv,
                   g_attn_out, bf(w_out), g_xattn, bf(w_q_mem), bf(w_o_mem),
                   g_mlp, bf(w_up), bf(w_down), g_final, seq, mem_len)
    return y.reshape(b, seq, d)
```

```python
import functools
import math

import jax
import jax.numpy as jnp
from jax.experimental import pallas as pl
from jax.experimental.pallas import tpu as pltpu

D_MODEL = 1024
ATT_WIDTH = 512
HEAD_DIM = 64
N_ATT_HEADS = 8
CONV_WIDTH = 512
CONV_K = 3
DILATED_PATTERNS = ((128, 1), (512, 4), (2048, 16))
BAND_STEPS = 128
N_MEM_HEADS = 4
MEM_HEAD_DIM = 256
D_FF = 4096
NORM_EPS = 1e-6
NEG_INF = -1e30

LANES = 128
F32_SUBLANES = 8
PAIR_LANES = 2 * HEAD_DIM
assert PAIR_LANES == LANES
N_SLABS = ATT_WIDTH // LANES
VMEM_LIMIT_BYTES = 56 * 1024 * 1024

TOKEN_TILE = 512
POST_TILE = 256
FF_CHUNK = 1024
MLP_CHUNKS_PER_STAGE = (2, 1, 1)
ATTN_SUBTILES = 4
RELAYOUT_RADIX = 4


def _rms(x, g):
    ms = jnp.mean(x * x, axis=-1, keepdims=True)
    return x * jax.lax.rsqrt(ms + NORM_EPS) * g


def _dot(a, b):
    return jnp.dot(a, b, preferred_element_type=jnp.float32)


def _dot_nt(a, b):
    return jax.lax.dot_general(a, b, (((1,), (1,)), ((), ())),
                               preferred_element_type=jnp.float32)


def _params(n_axes):
    return pltpu.CompilerParams(
        dimension_semantics=("arbitrary",) * n_axes,
        vmem_limit_bytes=VMEM_LIMIT_BYTES)


def _resident(shape):
    return pl.BlockSpec(shape, lambda *_: (0,) * len(shape), pipeline_mode=pl.Buffered(1))


def _slab(j):
    return slice(j * LANES, (j + 1) * LANES)


def _split_streams(val, out1_ref, out4_ref, out16_ref, nat_ref, by4_ref):
    tm = val.shape[0]
    n4, n16 = tm // 4, tm // 16
    out1_ref[...] = val.astype(jnp.bfloat16)
    for j in range(N_SLABS):
        nat_ref[j] = val[:, _slab(j)]
    for r in range(RELAYOUT_RADIX):
        for j in range(N_SLABS):
            piece = nat_ref[j, pl.ds(r, n4, stride=RELAYOUT_RADIX), :]
            by4_ref[j, r * n4:(r + 1) * n4, :] = piece
            out4_ref[r, :, _slab(j)] = piece.astype(jnp.bfloat16)
    for r4 in range(RELAYOUT_RADIX):
        for a in range(RELAYOUT_RADIX):
            for j in range(N_SLABS):
                piece = by4_ref[j, pl.ds(r4 * n4 + a, n16, stride=RELAYOUT_RADIX), :]
                out16_ref[RELAYOUT_RADIX * a + r4, :, _slab(j)] = piece.astype(jnp.bfloat16)


def _merge_streams4(in4_ref, nat_ref):
    n4 = in4_ref.shape[1]
    for r in range(RELAYOUT_RADIX):
        for j in range(N_SLABS):
            nat_ref[j, pl.ds(r, n4, stride=RELAYOUT_RADIX), :] = in4_ref[r, :, _slab(j)]


def _merge_streams16(in16_ref, by4_ref, nat_ref):
    n16 = in16_ref.shape[1]
    n4 = RELAYOUT_RADIX * n16
    for r4 in range(RELAYOUT_RADIX):
        for a in range(RELAYOUT_RADIX):
            for j in range(N_SLABS):
                by4_ref[j, pl.ds(r4 * n4 + a, n16, stride=RELAYOUT_RADIX), :] = (
                    in16_ref[RELAYOUT_RADIX * a + r4, :, _slab(j)])
    for r4 in range(RELAYOUT_RADIX):
        for j in range(N_SLABS):
            nat_ref[j, pl.ds(r4, n4, stride=RELAYOUT_RADIX), :] = (
                by4_ref[j, r4 * n4:(r4 + 1) * n4, :])


def _in_proj_kernel(tiles_per_seq, x_ref, g_ref, w_ref, cw_ref, gc_ref,
                    q1_ref, k1_ref, v1_ref, q4_ref, k4_ref, v4_ref,
                    q16_ref, k16_ref, v16_ref, c_ref, ubuf_ref, nat_ref, by4_ref):
    tm = x_ref.shape[0]
    h = _rms(x_ref[...], g_ref[...]).astype(jnp.bfloat16)

    def proj(col):
        return _dot(h, w_ref[:, col * ATT_WIDTH:(col + 1) * ATT_WIDTH])

    _split_streams(proj(0) * (1.0 / math.sqrt(HEAD_DIM)), q1_ref, q4_ref, q16_ref,
                   nat_ref, by4_ref)
    _split_streams(proj(1), k1_ref, k4_ref, k16_ref, nat_ref, by4_ref)
    _split_streams(proj(2), v1_ref, v4_ref, v16_ref, nat_ref, by4_ref)

    @pl.when(pl.program_id(0) % tiles_per_seq == 0)
    def _():
        ubuf_ref[tm:tm + F32_SUBLANES, :] = jnp.zeros((F32_SUBLANES, CONV_WIDTH), jnp.float32)

    ubuf_ref[0:F32_SUBLANES, :] = ubuf_ref[tm:tm + F32_SUBLANES, :]
    u = proj(4) * proj(5)
    ubuf_ref[F32_SUBLANES:tm + F32_SUBLANES, :] = u
    conv = (u * cw_ref[2:3, :]
            + ubuf_ref[F32_SUBLANES - 1:tm + F32_SUBLANES - 1, :] * cw_ref[1:2, :]
            + ubuf_ref[F32_SUBLANES - 2:tm + F32_SUBLANES - 2, :] * cw_ref[0:1, :])
    y = proj(3) * conv
    c_ref[...] = _rms(y, gc_ref[...]).astype(jnp.bfloat16)


def _stream_spec(dilation, tm, tiles_per_seq):
    return pl.BlockSpec((None, dilation, tm // dilation, ATT_WIDTH),
                        lambda i: (i // tiles_per_seq, 0, i % tiles_per_seq, 0))


def _in_proj(x2d, g_mix, w_in, conv_w, g_conv_out, batch, seq_len):
    t = x2d.shape[0]
    tm = TOKEN_TILE
    assert seq_len % tm == 0 and t == batch * seq_len
    tiles_per_seq = seq_len // tm
    nat = jax.ShapeDtypeStruct((t, ATT_WIDTH), jnp.bfloat16)
    by = lambda d: jax.ShapeDtypeStruct((batch, d, seq_len // d, ATT_WIDTH), jnp.bfloat16)
    row_spec = pl.BlockSpec((tm, ATT_WIDTH), lambda i: (i, 0))
    s4, s16 = _stream_spec(4, tm, tiles_per_seq), _stream_spec(16, tm, tiles_per_seq)
    return pl.pallas_call(
        functools.partial(_in_proj_kernel, tiles_per_seq),
        grid=(t // tm,),
        in_specs=[
            pl.BlockSpec((tm, D_MODEL), lambda i: (i, 0)),
            _resident((1, D_MODEL)),
            _resident(w_in.shape),
            _resident(conv_w.shape),
            _resident((1, CONV_WIDTH)),
        ],
        out_specs=[row_spec] * 3 + [s4] * 3 + [s16] * 3 + [row_spec],
        out_shape=[nat] * 3 + [by(4)] * 3 + [by(16)] * 3 + [nat],
        scratch_shapes=[pltpu.VMEM((tm + F32_SUBLANES, CONV_WIDTH), jnp.float32),
                        pltpu.VMEM((N_SLABS, tm, LANES), jnp.float32),
                        pltpu.VMEM((N_SLABS, tm, LANES), jnp.float32)],
        compiler_params=_params(1),
        name="in_proj",
    )(x2d, g_mix.reshape(1, -1), w_in, conv_w, g_conv_out.reshape(1, -1))


def _attn_kernel(q_ref, kc_ref, kp_ref, vc_ref, vp_ref, o_ref, lse_ref):
    s = BAND_STEPS
    streams, row_blocks = q_ref.shape[0], q_ref.shape[1] // s
    has_prev = pl.program_id(2) > 0
    qi = jax.lax.broadcasted_iota(jnp.int32, (2 * s, 2 * s), 0) & (s - 1)
    kj = jax.lax.broadcasted_iota(jnp.int32, (2 * s, 2 * s), 1)
    band = (kj >= qi) & (kj <= qi + s)
    valid_first = band & ((kj >= s) | has_prev)
    first_head = jax.lax.broadcasted_iota(jnp.int32, (s, PAIR_LANES), 1) < HEAD_DIM

    for r in range(streams):
        for i in range(row_blocks):
            rows = slice(i * s, (i + 1) * s)
            for pair in range(N_ATT_HEADS // 2):
                lanes = _slab(pair)
                q2 = q_ref[r, rows, lanes]
                zero = jnp.zeros_like(q2)
                q = jnp.concatenate([jnp.where(first_head, q2, zero),
                                     jnp.where(first_head, zero, q2)], axis=0)
                if i == 0:
                    k_prev, v_prev, valid = kp_ref[r, :, lanes], vp_ref[r, :, lanes], valid_first
                else:
                    prev_rows = slice((i - 1) * s, i * s)
                    k_prev, v_prev = kc_ref[r, prev_rows, lanes], vc_ref[r, prev_rows, lanes]
                    valid = band
                k = jnp.concatenate([k_prev, kc_ref[r, rows, lanes]], axis=0)
                v = jnp.concatenate([v_prev, vc_ref[r, rows, lanes]], axis=0)
                sc = jnp.where(valid, _dot_nt(q, k), NEG_INF)
                m = jnp.max(sc, axis=-1, keepdims=True)
                p = jnp.exp(sc - m)
                l = jnp.sum(p, axis=-1, keepdims=True)
                pv = _dot(p.astype(jnp.bfloat16), v) / l
                o_ref[r, rows, lanes] = jnp.where(first_head, pv[:s], pv[s:])
                lse = jnp.broadcast_to(m + jnp.log(l), (2 * s, PAIR_LANES))
                lse_ref[r, rows, lanes] = jnp.where(first_head, lse[:s], lse[s:])


def _dilated_attention(q, k, v):
    b, d, steps, _ = q.shape
    row_blocks = min(ATTN_SUBTILES, steps // BAND_STEPS)
    streams = ATTN_SUBTILES // row_blocks
    tile_rows = row_blocks * BAND_STEPS
    assert steps % tile_rows == 0 and d % streams == 0
    cur = pl.BlockSpec((None, streams, tile_rows, ATT_WIDTH), lambda bi, g, n: (bi, g, n, 0))
    prev = pl.BlockSpec((None, streams, BAND_STEPS, ATT_WIDTH),
                        lambda bi, g, n: (bi, g, jnp.maximum(n * row_blocks - 1, 0), 0))
    out = jax.ShapeDtypeStruct(q.shape, jnp.float32)
    return pl.pallas_call(
        _attn_kernel,
        grid=(b, d // streams, steps // tile_rows),
        in_specs=[cur, cur, prev, cur, prev],
        out_specs=[cur, cur],
        out_shape=[out, out],
        compiler_params=_params(3),
        name=f"dilated_attn_d{d}",
    )(q, k, k, v, v)


def _post_attn_kernel(o1_ref, l1_ref, o4_ref, l4_ref, o16_ref, l16_ref, c_ref, x_ref,
                      ga_ref, wout_ref, gx_ref, wq_ref, kv_ref, wo_ref,
                      gm_ref, wu_ref, wd_ref, gf_ref, y_ref,
                      on4_ref, ln4_ref, on16_ref, ln16_ref, by4_ref, xmid_ref):
    i = pl.program_id(0)

    @pl.when(i == 0)
    def _():
        xmid_ref[...] = jnp.zeros(xmid_ref.shape, jnp.float32)

    assert sum(MLP_CHUNKS_PER_STAGE) * FF_CHUNK == D_FF
    y = xmid_ref[(i + 1) % 2]
    h_mlp = _rms(y, gm_ref[...]).astype(jnp.bfloat16)
    chunk_ids = iter(range(D_FF // FF_CHUNK))

    def mlp_stage(y, stage):
        for _ in range(MLP_CHUNKS_PER_STAGE[stage]):
            c = next(chunk_ids)
            cols = slice(c * FF_CHUNK, (c + 1) * FF_CHUNK)
            a = jnp.maximum(_dot(h_mlp, wu_ref[:, cols]), 0.0)
            y = y + _dot((a * a).astype(jnp.bfloat16), wd_ref[cols, :])
        return y

    _merge_streams4(o4_ref, on4_ref)
    _merge_streams4(l4_ref, ln4_ref)
    _merge_streams16(o16_ref, by4_ref, on16_ref)
    _merge_streams16(l16_ref, by4_ref, ln16_ref)
    slabs = []
    for j in range(N_SLABS):
        lses = [l1_ref[:, _slab(j)], ln4_ref[j], ln16_ref[j]]
        outs = [o1_ref[:, _slab(j)], on4_ref[j], on16_ref[j]]
        top = jnp.maximum(jnp.maximum(lses[0], lses[1]), lses[2])
        es = [jnp.exp(l - top) for l in lses]
        num = es[0] * outs[0] + es[1] * outs[1] + es[2] * outs[2]
        slabs.append(num / (es[0] + es[1] + es[2]))
    attn = jnp.concatenate(slabs, axis=-1)
    attn_n = _rms(attn, ga_ref[...]).astype(jnp.bfloat16)
    y = mlp_stage(y, 0)
    x = (x_ref[...] + _dot(attn_n, wout_ref[0:ATT_WIDTH, :])
         + _dot(c_ref[...], wout_ref[ATT_WIDTH:, :]))

    y = mlp_stage(y, 1)

    h = _rms(x, gx_ref[...]).astype(jnp.bfloat16)
    q = (_dot(h, wq_ref[...]) * (1.0 / math.sqrt(MEM_HEAD_DIM))).astype(jnp.bfloat16)
    head = lambda hd: slice(hd * MEM_HEAD_DIM, (hd + 1) * MEM_HEAD_DIM)
    scores = [_dot_nt(q[:, head(hd)], kv_ref[:, head(hd)]) for hd in range(N_MEM_HEADS)]

    y = mlp_stage(y, 2)
    y_ref[...] = _rms(y, gf_ref[...])

    heads = []
    for hd in range(N_MEM_HEADS):
        sc = scores[hd]
        m = jnp.max(sc, axis=-1, keepdims=True)
        p = jnp.exp(sc - m)
        l = jnp.sum(p, axis=-1, keepdims=True)
        v = kv_ref[:, D_MODEL + hd * MEM_HEAD_DIM:D_MODEL + (hd + 1) * MEM_HEAD_DIM]
        heads.append((_dot(p.astype(jnp.bfloat16), v) / l).astype(jnp.bfloat16))
    xmid_ref[i % 2] = x + _dot(jnp.concatenate(heads, axis=-1), wo_ref[...])


def _post_attn(o1, l1, o4, l4, o16, l16, convn, x2d, kv, g_attn_out, w_out, g_xattn, w_q, w_o,
               g_mlp, w_up, w_down, g_final, seq_len, mem_len):
    t = x2d.shape[0]
    tm = POST_TILE
    assert seq_len % tm == 0
    tiles_per_seq = seq_len // tm
    n_tiles = t // tm
    tile_in = lambda i: jnp.minimum(i, n_tiles - 1)
    row_spec = pl.BlockSpec((tm, ATT_WIDTH), lambda i: (tile_in(i), 0))
    stream_spec = lambda d: pl.BlockSpec(
        (None, d, tm // d, ATT_WIDTH),
        lambda i: (tile_in(i) // tiles_per_seq, 0, tile_in(i) % tiles_per_seq, 0))
    s4, s16 = stream_spec(4), stream_spec(16)
    x_spec = pl.BlockSpec((tm, D_MODEL), lambda i: (tile_in(i), 0))
    gain = lambda g: g.reshape(1, -1)
    nat = pltpu.VMEM((N_SLABS, tm, LANES), jnp.float32)
    return pl.pallas_call(
        _post_attn_kernel,
        grid=(n_tiles + 1,),
        in_specs=[row_spec, row_spec, s4, s4, s16, s16, row_spec, x_spec,
                  _resident((1, ATT_WIDTH)), _resident(w_out.shape),
                  _resident((1, D_MODEL)), _resident(w_q.shape),
                  pl.BlockSpec((mem_len, 2 * D_MODEL),
                               lambda i: (tile_in(i) // tiles_per_seq, 0)),
                  _resident(w_o.shape),
                  _resident((1, D_MODEL)), _resident(w_up.shape), _resident(w_down.shape),
                  _resident((1, D_MODEL))],
        out_specs=pl.BlockSpec((tm, D_MODEL), lambda i: (jnp.maximum(i - 1, 0), 0)),
        out_shape=jax.ShapeDtypeStruct((t, D_MODEL), jnp.float32),
        scratch_shapes=[nat] * 5 + [pltpu.VMEM((2, tm, D_MODEL), jnp.float32)],
        compiler_params=_params(1),
        name="post_attn",
    )(o1, l1, o4, l4, o16, l16, convn, x2d, gain(g_attn_out), w_out, gain(g_xattn), w_q, kv, w_o,
      gain(g_mlp), w_up, w_down, gain(g_final))


def _mem_kv_kernel(m_ref, g_ref, w_ref, kv_ref):
    h = _rms(m_ref[...], g_ref[...]).astype(jnp.bfloat16)
    kv_ref[...] = _dot(h, w_ref[...]).astype(jnp.bfloat16)


def _mem_kv(mem2d, g_mem, w_kv):
    rows = mem2d.shape[0]
    tm = TOKEN_TILE
    return pl.pallas_call(
        _mem_kv_kernel,
        grid=(rows // tm,),
        in_specs=[pl.BlockSpec((tm, D_MODEL), lambda i: (i, 0)),
                  _resident((1, D_MODEL)), _resident(w_kv.shape)],
        out_specs=pl.BlockSpec((tm, 2 * D_MODEL), lambda i: (i, 0)),
        out_shape=jax.ShapeDtypeStruct((rows, 2 * D_MODEL), jnp.bfloat16),
        compiler_params=_params(1),
        name="mem_kv_proj",
    )(mem2d, g_mem.reshape(1, -1), w_kv)


def kernel(x, mem, g_mix, w_in, conv_w, g_attn_out, g_conv_out, w_out, g_xattn, g_mem,
           w_q_mem, w_kv_mem, w_o_mem, g_mlp, w_up, w_down, g_final):
    b, seq, d = x.shape
    mem_len = mem.shape[1]
    assert tuple(dil for _, dil in DILATED_PATTERNS) == (1, 4, 16)
    assert all(win // dil == BAND_STEPS for win, dil in DILATED_PATTERNS)
    bf = lambda w: w.astype(jnp.bfloat16)
    x2d = x.reshape(b * seq, d)

    (q1, k1, v1, q4, k4, v4, q16, k16, v16, convn) = _in_proj(
        x2d, g_mix, bf(w_in), conv_w, g_conv_out, b, seq)
    as_stream = lambda a: a.reshape(b, 1, seq, ATT_WIDTH)
    o1, l1 = _dilated_attention(as_stream(q1), as_stream(k1), as_stream(v1))
    o4, l4 = _dilated_attention(q4, k4, v4)
    o16, l16 = _dilated_attention(q16, k16, v16)
    flat = lambda a: a.reshape(b * seq, ATT_WIDTH)
    kv = _mem_kv(mem.reshape(b * mem_len, d), g_mem, bf(w_kv_mem))
    y = _post_attn(flat(o1), flat(l1), o4, l4, o16, l16, convn, x2d, kv,
                   g_attn_out, bf(w_out), g_xattn, bf(w_q_mem), bf(w_o_mem),
                   g_mlp, bf(w_up), bf(w_down), g_final, seq, mem_len)
    return y.reshape(b, seq, d)
```

```python
import functools
import math

import jax
import jax.numpy as jnp
from jax.experimental import pallas as pl
from jax.experimental.pallas import tpu as pltpu

D_MODEL = 1024
ATT_WIDTH = 512
HEAD_DIM = 64
N_ATT_HEADS = 8
CONV_WIDTH = 512
CONV_K = 3
DILATED_PATTERNS = ((128, 1), (512, 4), (2048, 16))
BAND_STEPS = 128
N_MEM_HEADS = 4
MEM_HEAD_DIM = 256
D_FF = 4096
NORM_EPS = 1e-6
NEG_INF = -1e30
LOG2_E = math.log2(math.e)

LANES = 128
F32_SUBLANES = 8
PAIR_LANES = 2 * HEAD_DIM
assert PAIR_LANES == LANES
N_SLABS = ATT_WIDTH // LANES
VMEM_LIMIT_BYTES = 56 * 1024 * 1024

TOKEN_TILE = 512
POST_TILE = 256
FF_CHUNK = 1024
MLP_CHUNKS_PER_STAGE = (2, 1, 1)
ATTN_SUBTILES = 4
RELAYOUT_RADIX = 4


def _rms(x, g):
    ms = jnp.mean(x * x, axis=-1, keepdims=True)
    return x * jax.lax.rsqrt(ms + NORM_EPS) * g


def _dot(a, b):
    return jnp.dot(a, b, preferred_element_type=jnp.float32)


def _dot_nt(a, b):
    return jax.lax.dot_general(a, b, (((1,), (1,)), ((), ())),
                               preferred_element_type=jnp.float32)


def _params(n_axes):
    return pltpu.CompilerParams(
        dimension_semantics=("arbitrary",) * n_axes,
        vmem_limit_bytes=VMEM_LIMIT_BYTES)


def _resident(shape):
    return pl.BlockSpec(shape, lambda *_: (0,) * len(shape), pipeline_mode=pl.Buffered(1))


def _slab(j):
    return slice(j * LANES, (j + 1) * LANES)


def _split_streams(val, out1_ref, out4_ref, out16_ref, nat_ref, by4_ref):
    tm = val.shape[0]
    n4, n16 = tm // 4, tm // 16
    out1_ref[...] = val.astype(jnp.bfloat16)
    for j in range(N_SLABS):
        nat_ref[j] = val[:, _slab(j)]
    for r in range(RELAYOUT_RADIX):
        for j in range(N_SLABS):
            piece = nat_ref[j, pl.ds(r, n4, stride=RELAYOUT_RADIX), :]
            by4_ref[j, r * n4:(r + 1) * n4, :] = piece
            out4_ref[r, :, _slab(j)] = piece.astype(jnp.bfloat16)
    for r4 in range(RELAYOUT_RADIX):
        for a in range(RELAYOUT_RADIX):
            for j in range(N_SLABS):
                piece = by4_ref[j, pl.ds(r4 * n4 + a, n16, stride=RELAYOUT_RADIX), :]
                out16_ref[RELAYOUT_RADIX * a + r4, :, _slab(j)] = piece.astype(jnp.bfloat16)


def _merge_streams4(in4_ref, nat_ref):
    n4 = in4_ref.shape[1]
    for r in range(RELAYOUT_RADIX):
        for j in range(N_SLABS):
            nat_ref[j, pl.ds(r, n4, stride=RELAYOUT_RADIX), :] = in4_ref[r, :, _slab(j)]


def _merge_streams16(in16_ref, by4_ref, nat_ref):
    n16 = in16_ref.shape[1]
    n4 = RELAYOUT_RADIX * n16
    for r4 in range(RELAYOUT_RADIX):
        for a in range(RELAYOUT_RADIX):
            for j in range(N_SLABS):
                by4_ref[j, pl.ds(r4 * n4 + a, n16, stride=RELAYOUT_RADIX), :] = (
                    in16_ref[RELAYOUT_RADIX * a + r4, :, _slab(j)])
    for r4 in range(RELAYOUT_RADIX):
        for j in range(N_SLABS):
            nat_ref[j, pl.ds(r4, n4, stride=RELAYOUT_RADIX), :] = (
                by4_ref[j, r4 * n4:(r4 + 1) * n4, :])


def _in_proj_kernel(tiles_per_seq, x_ref, g_ref, w_ref, cw_ref, gc_ref,
                    q1_ref, k1_ref, v1_ref, q4_ref, k4_ref, v4_ref,
                    q16_ref, k16_ref, v16_ref, c_ref, ubuf_ref, nat_ref, by4_ref):
    tm = x_ref.shape[0]
    h = _rms(x_ref[...], g_ref[...]).astype(jnp.bfloat16)

    def proj(col):
        return _dot(h, w_ref[:, col * ATT_WIDTH:(col + 1) * ATT_WIDTH])

    _split_streams(proj(0) * (LOG2_E / math.sqrt(HEAD_DIM)), q1_ref, q4_ref, q16_ref,
                   nat_ref, by4_ref)
    _split_streams(proj(1), k1_ref, k4_ref, k16_ref, nat_ref, by4_ref)
    _split_streams(proj(2), v1_ref, v4_ref, v16_ref, nat_ref, by4_ref)

    @pl.when(pl.program_id(0) % tiles_per_seq == 0)
    def _():
        ubuf_ref[tm:tm + F32_SUBLANES, :] = jnp.zeros((F32_SUBLANES, CONV_WIDTH), jnp.float32)

    ubuf_ref[0:F32_SUBLANES, :] = ubuf_ref[tm:tm + F32_SUBLANES, :]
    u = proj(4) * proj(5)
    ubuf_ref[F32_SUBLANES:tm + F32_SUBLANES, :] = u
    conv = (u * cw_ref[2:3, :]
            + ubuf_ref[F32_SUBLANES - 1:tm + F32_SUBLANES - 1, :] * cw_ref[1:2, :]
            + ubuf_ref[F32_SUBLANES - 2:tm + F32_SUBLANES - 2, :] * cw_ref[0:1, :])
    y = proj(3) * conv
    c_ref[...] = _rms(y, gc_ref[...]).astype(jnp.bfloat16)


def _stream_spec(dilation, tm, tiles_per_seq):
    return pl.BlockSpec((None, dilation, tm // dilation, ATT_WIDTH),
                        lambda i: (i // tiles_per_seq, 0, i % tiles_per_seq, 0))


def _in_proj(x2d, g_mix, w_in, conv_w, g_conv_out, batch, seq_len):
    t = x2d.shape[0]
    tm = TOKEN_TILE
    assert seq_len % tm == 0 and t == batch * seq_len
    tiles_per_seq = seq_len // tm
    nat = jax.ShapeDtypeStruct((t, ATT_WIDTH), jnp.bfloat16)
    by = lambda d: jax.ShapeDtypeStruct((batch, d, seq_len // d, ATT_WIDTH), jnp.bfloat16)
    row_spec = pl.BlockSpec((tm, ATT_WIDTH), lambda i: (i, 0))
    s4, s16 = _stream_spec(4, tm, tiles_per_seq), _stream_spec(16, tm, tiles_per_seq)
    return pl.pallas_call(
        functools.partial(_in_proj_kernel, tiles_per_seq),
        grid=(t // tm,),
        in_specs=[
            pl.BlockSpec((tm, D_MODEL), lambda i: (i, 0)),
            _resident((1, D_MODEL)),
            _resident(w_in.shape),
            _resident(conv_w.shape),
            _resident((1, CONV_WIDTH)),
        ],
        out_specs=[row_spec] * 3 + [s4] * 3 + [s16] * 3 + [row_spec],
        out_shape=[nat] * 3 + [by(4)] * 3 + [by(16)] * 3 + [nat],
        scratch_shapes=[pltpu.VMEM((tm + F32_SUBLANES, CONV_WIDTH), jnp.float32),
                        pltpu.VMEM((N_SLABS, tm, LANES), jnp.float32),
                        pltpu.VMEM((N_SLABS, tm, LANES), jnp.float32)],
        compiler_params=_params(1),
        name="in_proj",
    )(x2d, g_mix.reshape(1, -1), w_in, conv_w, g_conv_out.reshape(1, -1))


def _attn_kernel(q_ref, kc_ref, kp_ref, vc_ref, vp_ref, o_ref, lse_ref):
    s = BAND_STEPS
    streams, row_blocks = q_ref.shape[0], q_ref.shape[1] // s
    has_prev = pl.program_id(2) > 0
    qi = jax.lax.broadcasted_iota(jnp.int32, (2 * s, 2 * s), 0) & (s - 1)
    kj = jax.lax.broadcasted_iota(jnp.int32, (2 * s, 2 * s), 1)
    band = (kj >= qi) & (kj <= qi + s)
    valid_first = band & ((kj >= s) | has_prev)
    first_head = jax.lax.broadcasted_iota(jnp.int32, (s, PAIR_LANES), 1) < HEAD_DIM
    ones = jnp.ones((2 * s, PAIR_LANES), jnp.bfloat16)

    for r in range(streams):
        for i in range(row_blocks):
            rows = slice(i * s, (i + 1) * s)
            for pair in range(N_ATT_HEADS // 2):
                lanes = _slab(pair)
                q2 = q_ref[r, rows, lanes]
                zero = jnp.zeros_like(q2)
                q = jnp.concatenate([jnp.where(first_head, q2, zero),
                                     jnp.where(first_head, zero, q2)], axis=0)
                if i == 0:
                    k_prev, v_prev, valid = kp_ref[r, :, lanes], vp_ref[r, :, lanes], valid_first
                else:
                    prev_rows = slice((i - 1) * s, i * s)
                    k_prev, v_prev = kc_ref[r, prev_rows, lanes], vc_ref[r, prev_rows, lanes]
                    valid = band
                k = jnp.concatenate([k_prev, kc_ref[r, rows, lanes]], axis=0)
                v = jnp.concatenate(
                    [jnp.concatenate([v_prev, vc_ref[r, rows, lanes]], axis=0), ones], axis=1)
                sc = jnp.where(valid, _dot_nt(q, k), NEG_INF)
                m = jnp.max(sc, axis=-1, keepdims=True)
                p = jnp.exp2(sc - m)
                pv = _dot(p.astype(jnp.bfloat16), v)
                l = pv[:, PAIR_LANES:]
                o = pv[:, :PAIR_LANES] / l
                o_ref[r, rows, lanes] = jnp.where(first_head, o[:s], o[s:])
                lse = m + jnp.log2(l)
                lse_ref[r, rows, lanes] = jnp.where(first_head, lse[:s], lse[s:])


def _dilated_attention(q, k, v):
    b, d, steps, _ = q.shape
    row_blocks = min(ATTN_SUBTILES, steps // BAND_STEPS)
    streams = ATTN_SUBTILES // row_blocks
    tile_rows = row_blocks * BAND_STEPS
    assert steps % tile_rows == 0 and d % streams == 0
    cur = pl.BlockSpec((None, streams, tile_rows, ATT_WIDTH), lambda bi, g, n: (bi, g, n, 0))
    prev = pl.BlockSpec((None, streams, BAND_STEPS, ATT_WIDTH),
                        lambda bi, g, n: (bi, g, jnp.maximum(n * row_blocks - 1, 0), 0))
    out = jax.ShapeDtypeStruct(q.shape, jnp.float32)
    return pl.pallas_call(
        _attn_kernel,
        grid=(b, d // streams, steps // tile_rows),
        in_specs=[cur, cur, prev, cur, prev],
        out_specs=[cur, cur],
        out_shape=[out, out],
        compiler_params=_params(3),
        name=f"dilated_attn_d{d}",
    )(q, k, k, v, v)


def _post_attn_kernel(o1_ref, l1_ref, o4_ref, l4_ref, o16_ref, l16_ref, c_ref, x_ref,
                      ga_ref, wout_ref, gx_ref, wq_ref, kv_ref, wo_ref,
                      gm_ref, wu_ref, wd_ref, gf_ref, y_ref,
                      on4_ref, ln4_ref, on16_ref, ln16_ref, by4_ref, xmid_ref):
    i = pl.program_id(0)

    @pl.when(i == 0)
    def _():
        xmid_ref[...] = jnp.zeros(xmid_ref.shape, jnp.float32)

    assert sum(MLP_CHUNKS_PER_STAGE) * FF_CHUNK == D_FF
    y = xmid_ref[(i + 1) % 2]
    h_mlp = _rms(y, gm_ref[...]).astype(jnp.bfloat16)
    chunk_ids = iter(range(D_FF // FF_CHUNK))

    def mlp_stage(y, stage):
        for _ in range(MLP_CHUNKS_PER_STAGE[stage]):
            c = next(chunk_ids)
            cols = slice(c * FF_CHUNK, (c + 1) * FF_CHUNK)
            a = jnp.maximum(_dot(h_mlp, wu_ref[:, cols]), 0.0)
            y = y + _dot((a * a).astype(jnp.bfloat16), wd_ref[cols, :])
        return y

    _merge_streams4(o4_ref, on4_ref)
    _merge_streams4(l4_ref, ln4_ref)
    _merge_streams16(o16_ref, by4_ref, on16_ref)
    _merge_streams16(l16_ref, by4_ref, ln16_ref)
    slabs = []
    for j in range(N_SLABS):
        lses = [l1_ref[:, _slab(j)], ln4_ref[j], ln16_ref[j]]
        outs = [o1_ref[:, _slab(j)], on4_ref[j], on16_ref[j]]
        top = jnp.maximum(jnp.maximum(lses[0], lses[1]), lses[2])
        es = [jnp.exp2(l - top) for l in lses]
        num = es[0] * outs[0] + es[1] * outs[1] + es[2] * outs[2]
        slabs.append(num / (es[0] + es[1] + es[2]))
    attn = jnp.concatenate(slabs, axis=-1)
    attn_n = _rms(attn, ga_ref[...]).astype(jnp.bfloat16)
    y = mlp_stage(y, 0)
    x = (x_ref[...] + _dot(attn_n, wout_ref[0:ATT_WIDTH, :])
         + _dot(c_ref[...], wout_ref[ATT_WIDTH:, :]))

    y = mlp_stage(y, 1)

    h = _rms(x, gx_ref[...]).astype(jnp.bfloat16)
    q = (_dot(h, wq_ref[...]) * (LOG2_E / math.sqrt(MEM_HEAD_DIM))).astype(jnp.bfloat16)
    head = lambda hd: slice(hd * MEM_HEAD_DIM, (hd + 1) * MEM_HEAD_DIM)
    scores = [_dot_nt(q[:, head(hd)], kv_ref[:, head(hd)]) for hd in range(N_MEM_HEADS)]

    y = mlp_stage(y, 2)
    y_ref[...] = _rms(y, gf_ref[...])

    heads = []
    for hd in range(N_MEM_HEADS):
        sc = scores[hd]
        m = jnp.max(sc, axis=-1, keepdims=True)
        p = jnp.exp2(sc - m)
        l = jnp.sum(p, axis=-1, keepdims=True)
        v = kv_ref[:, D_MODEL + hd * MEM_HEAD_DIM:D_MODEL + (hd + 1) * MEM_HEAD_DIM]
        heads.append((_dot(p.astype(jnp.bfloat16), v) / l).astype(jnp.bfloat16))
    xmid_ref[i % 2] = x + _dot(jnp.concatenate(heads, axis=-1), wo_ref[...])


def _post_attn(o1, l1, o4, l4, o16, l16, convn, x2d, kv, g_attn_out, w_out, g_xattn, w_q, w_o,
               g_mlp, w_up, w_down, g_final, seq_len, mem_len):
    t = x2d.shape[0]
    tm = POST_TILE
    assert seq_len % tm == 0
    tiles_per_seq = seq_len // tm
    n_tiles = t // tm
    tile_in = lambda i: jnp.minimum(i, n_tiles - 1)
    row_spec = pl.BlockSpec((tm, ATT_WIDTH), lambda i: (tile_in(i), 0))
    stream_spec = lambda d: pl.BlockSpec(
        (None, d, tm // d, ATT_WIDTH),
        lambda i: (tile_in(i) // tiles_per_seq, 0, tile_in(i) % tiles_per_seq, 0))
    s4, s16 = stream_spec(4), stream_spec(16)
    x_spec = pl.BlockSpec((tm, D_MODEL), lambda i: (tile_in(i), 0))
    gain = lambda g: g.reshape(1, -1)
    nat = pltpu.VMEM((N_SLABS, tm, LANES), jnp.float32)
    return pl.pallas_call(
        _post_attn_kernel,
        grid=(n_tiles + 1,),
        in_specs=[row_spec, row_spec, s4, s4, s16, s16, row_spec, x_spec,
                  _resident((1, ATT_WIDTH)), _resident(w_out.shape),
                  _resident((1, D_MODEL)), _resident(w_q.shape),
                  pl.BlockSpec((mem_len, 2 * D_MODEL),
                               lambda i: (tile_in(i) // tiles_per_seq, 0)),
                  _resident(w_o.shape),
                  _resident((1, D_MODEL)), _resident(w_up.shape), _resident(w_down.shape),
                  _resident((1, D_MODEL))],
        out_specs=pl.BlockSpec((tm, D_MODEL), lambda i: (jnp.maximum(i - 1, 0), 0)),
        out_shape=jax.ShapeDtypeStruct((t, D_MODEL), jnp.float32),
        scratch_shapes=[nat] * 5 + [pltpu.VMEM((2, tm, D_MODEL), jnp.float32)],
        compiler_params=_params(1),
        name="post_attn",
    )(o1, l1, o4, l4, o16, l16, convn, x2d, gain(g_attn_out), w_out, gain(g_xattn), w_q, kv, w_o,
      gain(g_mlp), w_up, w_down, gain(g_final))


def _mem_kv_kernel(m_ref, g_ref, w_ref, kv_ref):
    h = _rms(m_ref[...], g_ref[...]).astype(jnp.bfloat16)
    kv_ref[...] = _dot(h, w_ref[...]).astype(jnp.bfloat16)


def _mem_kv(mem2d, g_mem, w_kv):
    rows = mem2d.shape[0]
    tm = TOKEN_TILE
    return pl.pallas_call(
        _mem_kv_kernel,
        grid=(rows // tm,),
        in_specs=[pl.BlockSpec((tm, D_MODEL), lambda i: (i, 0)),
                  _resident((1, D_MODEL)), _resident(w_kv.shape)],
        out_specs=pl.BlockSpec((tm, 2 * D_MODEL), lambda i: (i, 0)),
        out_shape=jax.ShapeDtypeStruct((rows, 2 * D_MODEL), jnp.bfloat16),
        compiler_params=_params(1),
        name="mem_kv_proj",
    )(mem2d, g_mem.reshape(1, -1), w_kv)


def kernel(x, mem, g_mix, w_in, conv_w, g_attn_out, g_conv_out, w_out, g_xattn, g_mem,
           w_q_mem, w_kv_mem, w_o_mem, g_mlp, w_up, w_down, g_final):
    b, seq, d = x.shape
    mem_len = mem.shape[1]
    assert tuple(dil for _, dil in DILATED_PATTERNS) == (1, 4, 16)
    assert all(win // dil == BAND_STEPS for win, dil in DILATED_PATTERNS)
    bf = lambda w: w.astype(jnp.bfloat16)
    x2d = x.reshape(b * seq, d)

    (q1, k1, v1, q4, k4, v4, q16, k16, v16, convn) = _in_proj(
        x2d, g_mix, bf(w_in), conv_w, g_conv_out, b, seq)
    as_stream = lambda a: a.reshape(b, 1, seq, ATT_WIDTH)
    o1, l1 = _dilated_attention(as_stream(q1), as_stream(k1), as_stream(v1))
    o4, l4 = _dilated_attention(q4, k4, v4)
    o16, l16 = _dilated_attention(q16, k16, v16)
    flat = lambda a: a.reshape(b * seq, ATT_WIDTH)
    kv = _mem_kv(mem.reshape(b * mem_len, d), g_mem, bf(w_kv_mem))
    y = _post_attn(flat(o1), flat(l1), o4, l4, o16, l16, convn, x2d, kv,
                   g_attn_out, bf(w_out), g_xattn, bf(w_q_mem), bf(w_o_mem),
                   g_mlp, bf(w_up), bf(w_down), g_final, seq, mem_len)
    return y.reshape(b, seq, d)
```

```python
import functools
import math

import jax
import jax.numpy as jnp
from jax.experimental import pallas as pl
from jax.experimental.pallas import tpu as pltpu

D_MODEL = 1024
ATT_WIDTH = 512
HEAD_DIM = 64
N_ATT_HEADS = 8
CONV_WIDTH = 512
CONV_K = 3
DILATED_PATTERNS = ((128, 1), (512, 4), (2048, 16))
BAND_STEPS = 128
N_MEM_HEADS = 4
MEM_HEAD_DIM = 256
D_FF = 4096
NORM_EPS = 1e-6
NEG_INF = -1e30
LOG2_E = math.log2(math.e)

LANES = 128
F32_SUBLANES = 8
PAIR_LANES = 2 * HEAD_DIM
assert PAIR_LANES == LANES
N_SLABS = ATT_WIDTH // LANES
VMEM_LIMIT_BYTES = 56 * 1024 * 1024

TOKEN_TILE = 512
POST_TILE = 256
FF_CHUNK = 1024
MLP_CHUNKS_PER_STAGE = (2, 1, 1)
ATTN_SUBTILES = 16
RELAYOUT_RADIX = 4


def _rms(x, g):
    ms = jnp.mean(x * x, axis=-1, keepdims=True)
    return x * jax.lax.rsqrt(ms + NORM_EPS) * g


def _dot(a, b):
    return jnp.dot(a, b, preferred_element_type=jnp.float32)


def _dot_nt(a, b):
    return jax.lax.dot_general(a, b, (((1,), (1,)), ((), ())),
                               preferred_element_type=jnp.float32)


def _params(n_axes):
    return pltpu.CompilerParams(
        dimension_semantics=("arbitrary",) * n_axes,
        vmem_limit_bytes=VMEM_LIMIT_BYTES)


def _resident(shape):
    return pl.BlockSpec(shape, lambda *_: (0,) * len(shape), pipeline_mode=pl.Buffered(1))


def _slab(j):
    return slice(j * LANES, (j + 1) * LANES)


def _split_streams(val, out1_ref, out4_ref, out16_ref, nat_ref, by4_ref):
    tm = val.shape[0]
    n4, n16 = tm // 4, tm // 16
    out1_ref[...] = val.astype(jnp.bfloat16)
    for j in range(N_SLABS):
        nat_ref[j] = val[:, _slab(j)]
    for r in range(RELAYOUT_RADIX):
        for j in range(N_SLABS):
            piece = nat_ref[j, pl.ds(r, n4, stride=RELAYOUT_RADIX), :]
            by4_ref[j, r * n4:(r + 1) * n4, :] = piece
            out4_ref[r, :, _slab(j)] = piece.astype(jnp.bfloat16)
    for r4 in range(RELAYOUT_RADIX):
        for a in range(RELAYOUT_RADIX):
            for j in range(N_SLABS):
                piece = by4_ref[j, pl.ds(r4 * n4 + a, n16, stride=RELAYOUT_RADIX), :]
                out16_ref[RELAYOUT_RADIX * a + r4, :, _slab(j)] = piece.astype(jnp.bfloat16)


def _merge_streams4(in4_ref, nat_ref):
    n4 = in4_ref.shape[1]
    for r in range(RELAYOUT_RADIX):
        for j in range(N_SLABS):
            nat_ref[j, pl.ds(r, n4, stride=RELAYOUT_RADIX), :] = in4_ref[r, :, _slab(j)]


def _merge_streams16(in16_ref, by4_ref, nat_ref):
    n16 = in16_ref.shape[1]
    n4 = RELAYOUT_RADIX * n16
    for r4 in range(RELAYOUT_RADIX):
        for a in range(RELAYOUT_RADIX):
            for j in range(N_SLABS):
                by4_ref[j, pl.ds(r4 * n4 + a, n16, stride=RELAYOUT_RADIX), :] = (
                    in16_ref[RELAYOUT_RADIX * a + r4, :, _slab(j)])
    for r4 in range(RELAYOUT_RADIX):
        for j in range(N_SLABS):
            nat_ref[j, pl.ds(r4, n4, stride=RELAYOUT_RADIX), :] = (
                by4_ref[j, r4 * n4:(r4 + 1) * n4, :])


def _in_proj_kernel(tiles_per_seq, x_ref, g_ref, w_ref, cw_ref, gc_ref,
                    q1_ref, k1_ref, v1_ref, q4_ref, k4_ref, v4_ref,
                    q16_ref, k16_ref, v16_ref, c_ref, ubuf_ref, nat_ref, by4_ref):
    tm = x_ref.shape[0]
    h = _rms(x_ref[...], g_ref[...]).astype(jnp.bfloat16)

    def proj(col):
        return _dot(h, w_ref[:, col * ATT_WIDTH:(col + 1) * ATT_WIDTH])

    _split_streams(proj(0) * (LOG2_E / math.sqrt(HEAD_DIM)), q1_ref, q4_ref, q16_ref,
                   nat_ref, by4_ref)
    _split_streams(proj(1), k1_ref, k4_ref, k16_ref, nat_ref, by4_ref)
    _split_streams(proj(2), v1_ref, v4_ref, v16_ref, nat_ref, by4_ref)

    @pl.when(pl.program_id(0) % tiles_per_seq == 0)
    def _():
        ubuf_ref[tm:tm + F32_SUBLANES, :] = jnp.zeros((F32_SUBLANES, CONV_WIDTH), jnp.float32)

    ubuf_ref[0:F32_SUBLANES, :] = ubuf_ref[tm:tm + F32_SUBLANES, :]
    u = proj(4) * proj(5)
    ubuf_ref[F32_SUBLANES:tm + F32_SUBLANES, :] = u
    conv = (u * cw_ref[2:3, :]
            + ubuf_ref[F32_SUBLANES - 1:tm + F32_SUBLANES - 1, :] * cw_ref[1:2, :]
            + ubuf_ref[F32_SUBLANES - 2:tm + F32_SUBLANES - 2, :] * cw_ref[0:1, :])
    y = proj(3) * conv
    c_ref[...] = _rms(y, gc_ref[...]).astype(jnp.bfloat16)


def _stream_spec(dilation, tm, tiles_per_seq):
    return pl.BlockSpec((None, dilation, tm // dilation, ATT_WIDTH),
                        lambda i: (i // tiles_per_seq, 0, i % tiles_per_seq, 0))


def _in_proj(x2d, g_mix, w_in, conv_w, g_conv_out, batch, seq_len):
    t = x2d.shape[0]
    tm = TOKEN_TILE
    assert seq_len % tm == 0 and t == batch * seq_len
    tiles_per_seq = seq_len // tm
    nat = jax.ShapeDtypeStruct((t, ATT_WIDTH), jnp.bfloat16)
    by = lambda d: jax.ShapeDtypeStruct((batch, d, seq_len // d, ATT_WIDTH), jnp.bfloat16)
    row_spec = pl.BlockSpec((tm, ATT_WIDTH), lambda i: (i, 0))
    s4, s16 = _stream_spec(4, tm, tiles_per_seq), _stream_spec(16, tm, tiles_per_seq)
    return pl.pallas_call(
        functools.partial(_in_proj_kernel, tiles_per_seq),
        grid=(t // tm,),
        in_specs=[
            pl.BlockSpec((tm, D_MODEL), lambda i: (i, 0)),
            _resident((1, D_MODEL)),
            _resident(w_in.shape),
            _resident(conv_w.shape),
            _resident((1, CONV_WIDTH)),
        ],
        out_specs=[row_spec] * 3 + [s4] * 3 + [s16] * 3 + [row_spec],
        out_shape=[nat] * 3 + [by(4)] * 3 + [by(16)] * 3 + [nat],
        scratch_shapes=[pltpu.VMEM((tm + F32_SUBLANES, CONV_WIDTH), jnp.float32),
                        pltpu.VMEM((N_SLABS, tm, LANES), jnp.float32),
                        pltpu.VMEM((N_SLABS, tm, LANES), jnp.float32)],
        compiler_params=_params(1),
        name="in_proj",
    )(x2d, g_mix.reshape(1, -1), w_in, conv_w, g_conv_out.reshape(1, -1))


def _attn_kernel(q_ref, kc_ref, kp_ref, vc_ref, vp_ref, o_ref, lse_ref):
    s = BAND_STEPS
    streams, row_blocks = q_ref.shape[0], q_ref.shape[1] // s
    has_prev = pl.program_id(2) > 0
    qi = jax.lax.broadcasted_iota(jnp.int32, (2 * s, 2 * s), 0) & (s - 1)
    kj = jax.lax.broadcasted_iota(jnp.int32, (2 * s, 2 * s), 1)
    band = (kj >= qi) & (kj <= qi + s)
    cap = jnp.where(band, jnp.inf, NEG_INF)
    cap_first = jnp.where(band & ((kj >= s) | has_prev), jnp.inf, NEG_INF)
    first_head = jax.lax.broadcasted_iota(jnp.int32, (s, PAIR_LANES), 1) < HEAD_DIM
    ones = jnp.ones((2 * s, PAIR_LANES), jnp.bfloat16)

    def rows_of(i):
        return slice(i * s, (i + 1) * s)

    def scores(r, i, pair):
        lanes = _slab(pair)
        q2 = q_ref[r, rows_of(i), lanes]
        zero = jnp.zeros_like(q2)
        q = jnp.concatenate([jnp.where(first_head, q2, zero),
                             jnp.where(first_head, zero, q2)], axis=0)
        k_prev = kp_ref[r, :, lanes] if i == 0 else kc_ref[r, rows_of(i - 1), lanes]
        k = jnp.concatenate([k_prev, kc_ref[r, rows_of(i), lanes]], axis=0)
        return jnp.minimum(_dot_nt(q, k), cap_first if i == 0 else cap)

    def finish(r, i, pair, sc):
        lanes = _slab(pair)
        v_prev = vp_ref[r, :, lanes] if i == 0 else vc_ref[r, rows_of(i - 1), lanes]
        v = jnp.concatenate(
            [jnp.concatenate([v_prev, vc_ref[r, rows_of(i), lanes]], axis=0), ones], axis=1)
        m = jnp.max(sc, axis=-1, keepdims=True)
        p = jnp.exp2(sc - m)
        pv = _dot(p.astype(jnp.bfloat16), v)
        l = pv[:, PAIR_LANES:]
        o = pv[:, :PAIR_LANES] / l
        o_ref[r, rows_of(i), lanes] = jnp.where(first_head, o[:s], o[s:])
        lse = m + jnp.log2(l)
        lse_ref[r, rows_of(i), lanes] = jnp.where(first_head, lse[:s], lse[s:])

    items = [(r, i, pair) for r in range(streams) for i in range(row_blocks)
             for pair in range(N_ATT_HEADS // 2)]
    sc = scores(*items[0])
    for n, item in enumerate(items):
        sc_next = scores(*items[n + 1]) if n + 1 < len(items) else None
        finish(*item, sc)
        sc = sc_next


def _dilated_attention(q, k, v):
    b, d, steps, _ = q.shape
    row_blocks = min(ATTN_SUBTILES, steps // BAND_STEPS)
    streams = ATTN_SUBTILES // row_blocks
    tile_rows = row_blocks * BAND_STEPS
    assert steps % tile_rows == 0 and d % streams == 0
    cur = pl.BlockSpec((None, streams, tile_rows, ATT_WIDTH), lambda bi, g, n: (bi, g, n, 0))
    prev = pl.BlockSpec((None, streams, BAND_STEPS, ATT_WIDTH),
                        lambda bi, g, n: (bi, g, jnp.maximum(n * row_blocks - 1, 0), 0))
    out = jax.ShapeDtypeStruct(q.shape, jnp.float32)
    return pl.pallas_call(
        _attn_kernel,
        grid=(b, d // streams, steps // tile_rows),
        in_specs=[cur, cur, prev, cur, prev],
        out_specs=[cur, cur],
        out_shape=[out, out],
        compiler_params=_params(3),
        name=f"dilated_attn_d{d}",
    )(q, k, k, v, v)


def _post_attn_kernel(o1_ref, l1_ref, o4_ref, l4_ref, o16_ref, l16_ref, c_ref, x_ref,
                      ga_ref, wout_ref, gx_ref, wq_ref, kv_ref, wo_ref,
                      gm_ref, wu_ref, wd_ref, gf_ref, y_ref,
                      on4_ref, ln4_ref, on16_ref, ln16_ref, by4_ref, xmid_ref):
    i = pl.program_id(0)

    @pl.when(i == 0)
    def _():
        xmid_ref[...] = jnp.zeros(xmid_ref.shape, jnp.float32)

    assert sum(MLP_CHUNKS_PER_STAGE) * FF_CHUNK == D_FF
    y = xmid_ref[(i + 1) % 2]
    h_mlp = _rms(y, gm_ref[...]).astype(jnp.bfloat16)
    chunk_ids = iter(range(D_FF // FF_CHUNK))

    def mlp_stage(y, stage):
        for _ in range(MLP_CHUNKS_PER_STAGE[stage]):
            c = next(chunk_ids)
            cols = slice(c * FF_CHUNK, (c + 1) * FF_CHUNK)
            a = jnp.maximum(_dot(h_mlp, wu_ref[:, cols]), 0.0)
            y = y + _dot((a * a).astype(jnp.bfloat16), wd_ref[cols, :])
        return y

    _merge_streams4(o4_ref, on4_ref)
    _merge_streams4(l4_ref, ln4_ref)
    _merge_streams16(o16_ref, by4_ref, on16_ref)
    _merge_streams16(l16_ref, by4_ref, ln16_ref)
    slabs = []
    for j in range(N_SLABS):
        lses = [l1_ref[:, _slab(j)], ln4_ref[j], ln16_ref[j]]
        outs = [o1_ref[:, _slab(j)], on4_ref[j], on16_ref[j]]
        top = jnp.maximum(jnp.maximum(lses[0], lses[1]), lses[2])
        es = [jnp.exp2(l - top) for l in lses]
        num = es[0] * outs[0] + es[1] * outs[1] + es[2] * outs[2]
        slabs.append(num / (es[0] + es[1] + es[2]))
    attn = jnp.concatenate(slabs, axis=-1)
    attn_n = _rms(attn, ga_ref[...]).astype(jnp.bfloat16)
    y = mlp_stage(y, 0)
    x = (x_ref[...] + _dot(attn_n, wout_ref[0:ATT_WIDTH, :])
         + _dot(c_ref[...], wout_ref[ATT_WIDTH:, :]))

    y = mlp_stage(y, 1)

    h = _rms(x, gx_ref[...]).astype(jnp.bfloat16)
    q = (_dot(h, wq_ref[...]) * (LOG2_E / math.sqrt(MEM_HEAD_DIM))).astype(jnp.bfloat16)
    head = lambda hd: slice(hd * MEM_HEAD_DIM, (hd + 1) * MEM_HEAD_DIM)
    scores = [_dot_nt(q[:, head(hd)], kv_ref[:, head(hd)]) for hd in range(N_MEM_HEADS)]

    y = mlp_stage(y, 2)
    y_ref[...] = _rms(y, gf_ref[...])

    heads = []
    for hd in range(N_MEM_HEADS):
        sc = scores[hd]
        m = jnp.max(sc, axis=-1, keepdims=True)
        p = jnp.exp2(sc - m)
        l = jnp.sum(p, axis=-1, keepdims=True)
        v = kv_ref[:, D_MODEL + hd * MEM_HEAD_DIM:D_MODEL + (hd + 1) * MEM_HEAD_DIM]
        heads.append((_dot(p.astype(jnp.bfloat16), v) / l).astype(jnp.bfloat16))
    xmid_ref[i % 2] = x + _dot(jnp.concatenate(heads, axis=-1), wo_ref[...])


def _post_attn(o1, l1, o4, l4, o16, l16, convn, x2d, kv, g_attn_out, w_out, g_xattn, w_q, w_o,
               g_mlp, w_up, w_down, g_final, seq_len, mem_len):
    t = x2d.shape[0]
    tm = POST_TILE
    assert seq_len % tm == 0
    tiles_per_seq = seq_len // tm
    n_tiles = t // tm
    tile_in = lambda i: jnp.minimum(i, n_tiles - 1)
    row_spec = pl.BlockSpec((tm, ATT_WIDTH), lambda i: (tile_in(i), 0))
    stream_spec = lambda d: pl.BlockSpec(
        (None, d, tm // d, ATT_WIDTH),
        lambda i: (tile_in(i) // tiles_per_seq, 0, tile_in(i) % tiles_per_seq, 0))
    s4, s16 = stream_spec(4), stream_spec(16)
    x_spec = pl.BlockSpec((tm, D_MODEL), lambda i: (tile_in(i), 0))
    gain = lambda g: g.reshape(1, -1)
    nat = pltpu.VMEM((N_SLABS, tm, LANES), jnp.float32)
    return pl.pallas_call(
        _post_attn_kernel,
        grid=(n_tiles + 1,),
        in_specs=[row_spec, row_spec, s4, s4, s16, s16, row_spec, x_spec,
                  _resident((1, ATT_WIDTH)), _resident(w_out.shape),
                  _resident((1, D_MODEL)), _resident(w_q.shape),
                  pl.BlockSpec((mem_len, 2 * D_MODEL),
                               lambda i: (tile_in(i) // tiles_per_seq, 0)),
                  _resident(w_o.shape),
                  _resident((1, D_MODEL)), _resident(w_up.shape), _resident(w_down.shape),
                  _resident((1, D_MODEL))],
        out_specs=pl.BlockSpec((tm, D_MODEL), lambda i: (jnp.maximum(i - 1, 0), 0)),
        out_shape=jax.ShapeDtypeStruct((t, D_MODEL), jnp.float32),
        scratch_shapes=[nat] * 5 + [pltpu.VMEM((2, tm, D_MODEL), jnp.float32)],
        compiler_params=_params(1),
        name="post_attn",
    )(o1, l1, o4, l4, o16, l16, convn, x2d, gain(g_attn_out), w_out, gain(g_xattn), w_q, kv, w_o,
      gain(g_mlp), w_up, w_down, gain(g_final))


def _mem_kv_kernel(m_ref, g_ref, w_ref, kv_ref):
    h = _rms(m_ref[...], g_ref[...]).astype(jnp.bfloat16)
    kv_ref[...] = _dot(h, w_ref[...]).astype(jnp.bfloat16)


def _mem_kv(mem2d, g_mem, w_kv):
    rows = mem2d.shape[0]
    tm = TOKEN_TILE
    return pl.pallas_call(
        _mem_kv_kernel,
        grid=(rows // tm,),
        in_specs=[pl.BlockSpec((tm, D_MODEL), lambda i: (i, 0)),
                  _resident((1, D_MODEL)), _resident(w_kv.shape)],
        out_specs=pl.BlockSpec((tm, 2 * D_MODEL), lambda i: (i, 0)),
        out_shape=jax.ShapeDtypeStruct((rows, 2 * D_MODEL), jnp.bfloat16),
        compiler_params=_params(1),
        name="mem_kv_proj",
    )(mem2d, g_mem.reshape(1, -1), w_kv)


def kernel(x, mem, g_mix, w_in, conv_w, g_attn_out, g_conv_out, w_out, g_xattn, g_mem,
           w_q_mem, w_kv_mem, w_o_mem, g_mlp, w_up, w_down, g_final):
    b, seq, d = x.shape
    mem_len = mem.shape[1]
    assert tuple(dil for _, dil in DILATED_PATTERNS) == (1, 4, 16)
    assert all(win // dil == BAND_STEPS for win, dil in DILATED_PATTERNS)
    bf = lambda w: w.astype(jnp.bfloat16)
    x2d = x.reshape(b * seq, d)

    (q1, k1, v1, q4, k4, v4, q16, k16, v16, convn) = _in_proj(
        x2d, g_mix, bf(w_in), conv_w, g_conv_out, b, seq)
    as_stream = lambda a: a.reshape(b, 1, seq, ATT_WIDTH)
    o1, l1 = _dilated_attention(as_stream(q1), as_stream(k1), as_stream(v1))
    o4, l4 = _dilated_attention(q4, k4, v4)
    o16, l16 = _dilated_attention(q16, k16, v16)
    flat = lambda a: a.reshape(b * seq, ATT_WIDTH)
    kv = _mem_kv(mem.reshape(b * mem_len, d), g_mem, bf(w_kv_mem))
    y = _post_attn(flat(o1), flat(l1), o4, l4, o16, l16, convn, x2d, kv,
                   g_attn_out, bf(w_out), g_xattn, bf(w_q_mem), bf(w_o_mem),
                   g_mlp, bf(w_up), bf(w_down), g_final, seq, mem_len)
    return y.reshape(b, seq, d)
```

```python
import functools
import math

import jax
import jax.numpy as jnp
from jax.experimental import pallas as pl
from jax.experimental.pallas import tpu as pltpu

D_MODEL = 1024
ATT_WIDTH = 512
HEAD_DIM = 64
N_ATT_HEADS = 8
CONV_WIDTH = 512
CONV_K = 3
DILATED_PATTERNS = ((128, 1), (512, 4), (2048, 16))
BAND_STEPS = 128
N_MEM_HEADS = 4
MEM_HEAD_DIM = 256
D_FF = 4096
NORM_EPS = 1e-6
NEG_INF = -1e30
LOG2_E = math.log2(math.e)

LANES = 128
F32_SUBLANES = 8
PAIR_LANES = 2 * HEAD_DIM
assert PAIR_LANES == LANES
N_SLABS = ATT_WIDTH // LANES
VMEM_LIMIT_BYTES = 56 * 1024 * 1024

TOKEN_TILE = 1024
POST_TILE = 256
FF_CHUNK = 1024
MLP_CHUNKS_PER_STAGE = (2, 1, 1)
ATTN_SUBTILES = 16
RELAYOUT_RADIX = 4


def _rms(x, g):
    ms = jnp.mean(x * x, axis=-1, keepdims=True)
    return x * jax.lax.rsqrt(ms + NORM_EPS) * g


def _dot(a, b):
    return jnp.dot(a, b, preferred_element_type=jnp.float32)


def _dot_nt(a, b):
    return jax.lax.dot_general(a, b, (((1,), (1,)), ((), ())),
                               preferred_element_type=jnp.float32)


def _params(n_axes):
    return pltpu.CompilerParams(
        dimension_semantics=("arbitrary",) * n_axes,
        vmem_limit_bytes=VMEM_LIMIT_BYTES)


def _resident(shape):
    return pl.BlockSpec(shape, lambda *_: (0,) * len(shape), pipeline_mode=pl.Buffered(1))


def _slab(j):
    return slice(j * LANES, (j + 1) * LANES)


def _split_streams(nat_ref, out4_ref, out16_ref, by4_ref):
    tm = nat_ref.shape[1]
    n4, n16 = tm // 4, tm // 16
    for r in range(RELAYOUT_RADIX):
        for j in range(N_SLABS):
            piece = nat_ref[j, pl.ds(r, n4, stride=RELAYOUT_RADIX), :]
            by4_ref[j, r * n4:(r + 1) * n4, :] = piece
            out4_ref[r, :, _slab(j)] = piece.astype(jnp.bfloat16)
    for r4 in range(RELAYOUT_RADIX):
        for a in range(RELAYOUT_RADIX):
            for j in range(N_SLABS):
                piece = by4_ref[j, pl.ds(r4 * n4 + a, n16, stride=RELAYOUT_RADIX), :]
                out16_ref[RELAYOUT_RADIX * a + r4, :, _slab(j)] = piece.astype(jnp.bfloat16)


def _merge_streams4(in4_ref, nat_ref):
    n4 = in4_ref.shape[1]
    for r in range(RELAYOUT_RADIX):
        for j in range(N_SLABS):
            nat_ref[j, pl.ds(r, n4, stride=RELAYOUT_RADIX), :] = in4_ref[r, :, _slab(j)]


def _merge_streams16(in16_ref, by4_ref, nat_ref):
    n16 = in16_ref.shape[1]
    n4 = RELAYOUT_RADIX * n16
    for r4 in range(RELAYOUT_RADIX):
        for a in range(RELAYOUT_RADIX):
            for j in range(N_SLABS):
                by4_ref[j, pl.ds(r4 * n4 + a, n16, stride=RELAYOUT_RADIX), :] = (
                    in16_ref[RELAYOUT_RADIX * a + r4, :, _slab(j)])
    for r4 in range(RELAYOUT_RADIX):
        for j in range(N_SLABS):
            nat_ref[j, pl.ds(r4, n4, stride=RELAYOUT_RADIX), :] = (
                by4_ref[j, r4 * n4:(r4 + 1) * n4, :])


def _in_proj_kernel(tiles_per_seq, x_ref, g_ref, w_ref, cw_ref, gc_ref,
                    q1_ref, k1_ref, v1_ref, q4_ref, k4_ref, v4_ref,
                    q16_ref, k16_ref, v16_ref, c_ref, ubuf_ref, nat_ref, by4_ref):
    tm = x_ref.shape[0]
    h = _rms(x_ref[...], g_ref[...]).astype(jnp.bfloat16)

    def proj(col):
        return _dot(h, w_ref[:, col * ATT_WIDTH:(col + 1) * ATT_WIDTH])

    def project_streams(col, out1_ref, out4_ref, out16_ref, scale=None):
        val = proj(col) if scale is None else proj(col) * scale
        out1_ref[...] = val.astype(jnp.bfloat16)
        for j in range(N_SLABS):
            nat_ref[j] = val[:, _slab(j)]
        _split_streams(nat_ref, out4_ref, out16_ref, by4_ref)

    project_streams(0, q1_ref, q4_ref, q16_ref, LOG2_E / math.sqrt(HEAD_DIM))
    project_streams(1, k1_ref, k4_ref, k16_ref)
    project_streams(2, v1_ref, v4_ref, v16_ref)

    @pl.when(pl.program_id(0) % tiles_per_seq == 0)
    def _():
        ubuf_ref[tm:tm + F32_SUBLANES, :] = jnp.zeros((F32_SUBLANES, CONV_WIDTH), jnp.float32)

    ubuf_ref[0:F32_SUBLANES, :] = ubuf_ref[tm:tm + F32_SUBLANES, :]
    u = proj(4) * proj(5)
    ubuf_ref[F32_SUBLANES:tm + F32_SUBLANES, :] = u
    conv = (u * cw_ref[2:3, :]
            + ubuf_ref[F32_SUBLANES - 1:tm + F32_SUBLANES - 1, :] * cw_ref[1:2, :]
            + ubuf_ref[F32_SUBLANES - 2:tm + F32_SUBLANES - 2, :] * cw_ref[0:1, :])
    y = proj(3) * conv
    c_ref[...] = _rms(y, gc_ref[...]).astype(jnp.bfloat16)


def _stream_spec(dilation, tm, tiles_per_seq):
    return pl.BlockSpec((None, dilation, tm // dilation, ATT_WIDTH),
                        lambda i: (i // tiles_per_seq, 0, i % tiles_per_seq, 0))


def _in_proj(x2d, g_mix, w_in, conv_w, g_conv_out, batch, seq_len):
    t = x2d.shape[0]
    tm = TOKEN_TILE
    assert seq_len % tm == 0 and t == batch * seq_len
    tiles_per_seq = seq_len // tm
    nat = jax.ShapeDtypeStruct((t, ATT_WIDTH), jnp.bfloat16)
    by = lambda d: jax.ShapeDtypeStruct((batch, d, seq_len // d, ATT_WIDTH), jnp.bfloat16)
    row_spec = pl.BlockSpec((tm, ATT_WIDTH), lambda i: (i, 0))
    s4, s16 = _stream_spec(4, tm, tiles_per_seq), _stream_spec(16, tm, tiles_per_seq)
    return pl.pallas_call(
        functools.partial(_in_proj_kernel, tiles_per_seq),
        grid=(t // tm,),
        in_specs=[
            pl.BlockSpec((tm, D_MODEL), lambda i: (i, 0)),
            _resident((1, D_MODEL)),
            _resident(w_in.shape),
            _resident(conv_w.shape),
            _resident((1, CONV_WIDTH)),
        ],
        out_specs=[row_spec] * 3 + [s4] * 3 + [s16] * 3 + [row_spec],
        out_shape=[nat] * 3 + [by(4)] * 3 + [by(16)] * 3 + [nat],
        scratch_shapes=[pltpu.VMEM((tm + F32_SUBLANES, CONV_WIDTH), jnp.float32),
                        pltpu.VMEM((N_SLABS, tm, LANES), jnp.float32),
                        pltpu.VMEM((N_SLABS, tm, LANES), jnp.float32)],
        compiler_params=_params(1),
        name="in_proj",
    )(x2d, g_mix.reshape(1, -1), w_in, conv_w, g_conv_out.reshape(1, -1))


def _attn_kernel(q_ref, kc_ref, kp_ref, vc_ref, vp_ref, o_ref, lse_ref):
    s = BAND_STEPS
    streams, row_blocks = q_ref.shape[0], q_ref.shape[1] // s
    has_prev = pl.program_id(2) > 0
    qi = jax.lax.broadcasted_iota(jnp.int32, (2 * s, 2 * s), 0) & (s - 1)
    kj = jax.lax.broadcasted_iota(jnp.int32, (2 * s, 2 * s), 1)
    band = (kj >= qi) & (kj <= qi + s)
    cap = jnp.where(band, jnp.inf, NEG_INF)
    cap_first = jnp.where(band & ((kj >= s) | has_prev), jnp.inf, NEG_INF)
    first_head = jax.lax.broadcasted_iota(jnp.int32, (s, PAIR_LANES), 1) < HEAD_DIM
    ones = jnp.ones((2 * s, PAIR_LANES), jnp.bfloat16)

    def rows_of(i):
        return slice(i * s, (i + 1) * s)

    def scores(r, i, pair):
        lanes = _slab(pair)
        q2 = q_ref[r, rows_of(i), lanes]
        zero = jnp.zeros_like(q2)
        q = jnp.concatenate([jnp.where(first_head, q2, zero),
                             jnp.where(first_head, zero, q2)], axis=0)
        k_prev = kp_ref[r, :, lanes] if i == 0 else kc_ref[r, rows_of(i - 1), lanes]
        k = jnp.concatenate([k_prev, kc_ref[r, rows_of(i), lanes]], axis=0)
        return jnp.minimum(_dot_nt(q, k), cap_first if i == 0 else cap)

    def finish(r, i, pair, sc):
        lanes = _slab(pair)
        v_prev = vp_ref[r, :, lanes] if i == 0 else vc_ref[r, rows_of(i - 1), lanes]
        v = jnp.concatenate(
            [jnp.concatenate([v_prev, vc_ref[r, rows_of(i), lanes]], axis=0), ones], axis=1)
        m = jnp.max(sc, axis=-1, keepdims=True)
        p = jnp.exp2(sc - m)
        pv = _dot(p.astype(jnp.bfloat16), v)
        l = pv[:, PAIR_LANES:]
        o = pv[:, :PAIR_LANES] / l
        o_ref[r, rows_of(i), lanes] = jnp.where(first_head, o[:s], o[s:])
        lse = m + jnp.log2(l)
        lse_ref[r, rows_of(i), lanes] = jnp.where(first_head, lse[:s], lse[s:])

    items = [(r, i, pair) for r in range(streams) for i in range(row_blocks)
             for pair in range(N_ATT_HEADS // 2)]
    sc = scores(*items[0])
    for n, item in enumerate(items):
        sc_next = scores(*items[n + 1]) if n + 1 < len(items) else None
        finish(*item, sc)
        sc = sc_next


def _dilated_attention(q, k, v):
    b, d, steps, _ = q.shape
    row_blocks = min(ATTN_SUBTILES, steps // BAND_STEPS)
    streams = ATTN_SUBTILES // row_blocks
    tile_rows = row_blocks * BAND_STEPS
    assert steps % tile_rows == 0 and d % streams == 0
    cur = pl.BlockSpec((None, streams, tile_rows, ATT_WIDTH), lambda bi, g, n: (bi, g, n, 0))
    prev = pl.BlockSpec((None, streams, BAND_STEPS, ATT_WIDTH),
                        lambda bi, g, n: (bi, g, jnp.maximum(n * row_blocks - 1, 0), 0))
    out = jax.ShapeDtypeStruct(q.shape, jnp.float32)
    return pl.pallas_call(
        _attn_kernel,
        grid=(b, d // streams, steps // tile_rows),
        in_specs=[cur, cur, prev, cur, prev],
        out_specs=[cur, cur],
        out_shape=[out, out],
        compiler_params=_params(3),
        name=f"dilated_attn_d{d}",
    )(q, k, k, v, v)


def _post_attn_kernel(o1_ref, l1_ref, o4_ref, l4_ref, o16_ref, l16_ref, c_ref, x_ref,
                      ga_ref, wout_ref, gx_ref, wq_ref, kv_ref, wo_ref,
                      gm_ref, wu_ref, wd_ref, gf_ref, y_ref,
                      on4_ref, ln4_ref, on16_ref, ln16_ref, by4_ref, xmid_ref):
    i = pl.program_id(0)

    @pl.when(i == 0)
    def _():
        xmid_ref[...] = jnp.zeros(xmid_ref.shape, jnp.float32)

    assert sum(MLP_CHUNKS_PER_STAGE) * FF_CHUNK == D_FF
    y = xmid_ref[(i + 1) % 2]
    h_mlp = _rms(y, gm_ref[...]).astype(jnp.bfloat16)
    chunk_ids = iter(range(D_FF // FF_CHUNK))

    def mlp_stage(y, stage):
        for _ in range(MLP_CHUNKS_PER_STAGE[stage]):
            c = next(chunk_ids)
            cols = slice(c * FF_CHUNK, (c + 1) * FF_CHUNK)
            a = jnp.maximum(_dot(h_mlp, wu_ref[:, cols]), 0.0)
            y = y + _dot((a * a).astype(jnp.bfloat16), wd_ref[cols, :])
        return y

    _merge_streams4(o4_ref, on4_ref)
    _merge_streams4(l4_ref, ln4_ref)
    _merge_streams16(o16_ref, by4_ref, on16_ref)
    _merge_streams16(l16_ref, by4_ref, ln16_ref)
    slabs = []
    for j in range(N_SLABS):
        lses = [l1_ref[:, _slab(j)], ln4_ref[j], ln16_ref[j]]
        outs = [o1_ref[:, _slab(j)], on4_ref[j], on16_ref[j]]
        top = jnp.maximum(jnp.maximum(lses[0], lses[1]), lses[2])
        es = [jnp.exp2(l - top) for l in lses]
        num = es[0] * outs[0] + es[1] * outs[1] + es[2] * outs[2]
        slabs.append(num / (es[0] + es[1] + es[2]))
    attn = jnp.concatenate(slabs, axis=-1)
    attn_n = _rms(attn, ga_ref[...]).astype(jnp.bfloat16)
    y = mlp_stage(y, 0)
    x = (x_ref[...] + _dot(attn_n, wout_ref[0:ATT_WIDTH, :])
         + _dot(c_ref[...], wout_ref[ATT_WIDTH:, :]))

    y = mlp_stage(y, 1)

    h = _rms(x, gx_ref[...]).astype(jnp.bfloat16)
    q = (_dot(h, wq_ref[...]) * (LOG2_E / math.sqrt(MEM_HEAD_DIM))).astype(jnp.bfloat16)
    head = lambda hd: slice(hd * MEM_HEAD_DIM, (hd + 1) * MEM_HEAD_DIM)
    scores = [_dot_nt(q[:, head(hd)], kv_ref[:, head(hd)]) for hd in range(N_MEM_HEADS)]

    y = mlp_stage(y, 2)
    y_ref[...] = _rms(y, gf_ref[...])

    heads = []
    for hd in range(N_MEM_HEADS):
        sc = scores[hd]
        m = jnp.max(sc, axis=-1, keepdims=True)
        p = jnp.exp2(sc - m)
        l = jnp.sum(p, axis=-1, keepdims=True)
        v = kv_ref[:, D_MODEL + hd * MEM_HEAD_DIM:D_MODEL + (hd + 1) * MEM_HEAD_DIM]
        heads.append((_dot(p.astype(jnp.bfloat16), v) / l).astype(jnp.bfloat16))
    xmid_ref[i % 2] = x + _dot(jnp.concatenate(heads, axis=-1), wo_ref[...])


def _post_attn(o1, l1, o4, l4, o16, l16, convn, x2d, kv, g_attn_out, w_out, g_xattn, w_q, w_o,
               g_mlp, w_up, w_down, g_final, seq_len, mem_len):
    t = x2d.shape[0]
    tm = POST_TILE
    assert seq_len % tm == 0
    tiles_per_seq = seq_len // tm
    n_tiles = t // tm
    tile_in = lambda i: jnp.minimum(i, n_tiles - 1)
    row_spec = pl.BlockSpec((tm, ATT_WIDTH), lambda i: (tile_in(i), 0))
    stream_spec = lambda d: pl.BlockSpec(
        (None, d, tm // d, ATT_WIDTH),
        lambda i: (tile_in(i) // tiles_per_seq, 0, tile_in(i) % tiles_per_seq, 0))
    s4, s16 = stream_spec(4), stream_spec(16)
    x_spec = pl.BlockSpec((tm, D_MODEL), lambda i: (tile_in(i), 0))
    gain = lambda g: g.reshape(1, -1)
    nat = pltpu.VMEM((N_SLABS, tm, LANES), jnp.float32)
    return pl.pallas_call(
        _post_attn_kernel,
        grid=(n_tiles + 1,),
        in_specs=[row_spec, row_spec, s4, s4, s16, s16, row_spec, x_spec,
                  _resident((1, ATT_WIDTH)), _resident(w_out.shape),
                  _resident((1, D_MODEL)), _resident(w_q.shape),
                  pl.BlockSpec((mem_len, 2 * D_MODEL),
                               lambda i: (tile_in(i) // tiles_per_seq, 0)),
                  _resident(w_o.shape),
                  _resident((1, D_MODEL)), _resident(w_up.shape), _resident(w_down.shape),
                  _resident((1, D_MODEL))],
        out_specs=pl.BlockSpec((tm, D_MODEL), lambda i: (jnp.maximum(i - 1, 0), 0)),
        out_shape=jax.ShapeDtypeStruct((t, D_MODEL), jnp.float32),
        scratch_shapes=[nat] * 5 + [pltpu.VMEM((2, tm, D_MODEL), jnp.float32)],
        compiler_params=_params(1),
        name="post_attn",
    )(o1, l1, o4, l4, o16, l16, convn, x2d, gain(g_attn_out), w_out, gain(g_xattn), w_q, kv, w_o,
      gain(g_mlp), w_up, w_down, gain(g_final))


def _mem_kv_kernel(m_ref, g_ref, w_ref, kv_ref):
    h = _rms(m_ref[...], g_ref[...]).astype(jnp.bfloat16)
    kv_ref[...] = _dot(h, w_ref[...]).astype(jnp.bfloat16)


def _mem_kv(mem2d, g_mem, w_kv):
    rows = mem2d.shape[0]
    tm = TOKEN_TILE
    return pl.pallas_call(
        _mem_kv_kernel,
        grid=(rows // tm,),
        in_specs=[pl.BlockSpec((tm, D_MODEL), lambda i: (i, 0)),
                  _resident((1, D_MODEL)), _resident(w_kv.shape)],
        out_specs=pl.BlockSpec((tm, 2 * D_MODEL), lambda i: (i, 0)),
        out_shape=jax.ShapeDtypeStruct((rows, 2 * D_MODEL), jnp.bfloat16),
        compiler_params=_params(1),
        name="mem_kv_proj",
    )(mem2d, g_mem.reshape(1, -1), w_kv)


def kernel(x, mem, g_mix, w_in, conv_w, g_attn_out, g_conv_out, w_out, g_xattn, g_mem,
           w_q_mem, w_kv_mem, w_o_mem, g_mlp, w_up, w_down, g_final):
    b, seq, d = x.shape
    mem_len = mem.shape[1]
    assert tuple(dil for _, dil in DILATED_PATTERNS) == (1, 4, 16)
    assert all(win // dil == BAND_STEPS for win, dil in DILATED_PATTERNS)
    bf = lambda w: w.astype(jnp.bfloat16)
    x2d = x.reshape(b * seq, d)

    (q1, k1, v1, q4, k4, v4, q16, k16, v16, convn) = _in_proj(
        x2d, g_mix, bf(w_in), conv_w, g_conv_out, b, seq)
    as_stream = lambda a: a.reshape(b, 1, seq, ATT_WIDTH)
    o1, l1 = _dilated_attention(as_stream(q1), as_stream(k1), as_stream(v1))
    o4, l4 = _dilated_attention(q4, k4, v4)
    o16, l16 = _dilated_attention(q16, k16, v16)
    flat = lambda a: a.reshape(b * seq, ATT_WIDTH)
    kv = _mem_kv(mem.reshape(b * mem_len, d), g_mem, bf(w_kv_mem))
    y = _post_attn(flat(o1), flat(l1), o4, l4, o16, l16, convn, x2d, kv,
                   g_attn_out, bf(w_out), g_xattn, bf(w_q_mem), bf(w_o_mem),
                   g_mlp, bf(w_up), bf(w_down), g_final, seq, mem_len)
    return y.reshape(b, seq, d)
```

```python
import functools
import math

import jax
import jax.numpy as jnp
from jax.experimental import pallas as pl
from jax.experimental.pallas import tpu as pltpu

D_MODEL = 1024
ATT_WIDTH = 512
HEAD_DIM = 64
N_ATT_HEADS = 8
CONV_WIDTH = 512
CONV_K = 3
DILATED_PATTERNS = ((128, 1), (512, 4), (2048, 16))
BAND_STEPS = 128
N_MEM_HEADS = 4
MEM_HEAD_DIM = 256
D_FF = 4096
NORM_EPS = 1e-6
NEG_INF = -1e30
LOG2_E = math.log2(math.e)

LANES = 128
F32_SUBLANES = 8
PAIR_LANES = 2 * HEAD_DIM
assert PAIR_LANES == LANES
N_SLABS = ATT_WIDTH // LANES
LSE_REP = LANES // N_ATT_HEADS
VMEM_LIMIT_BYTES = 56 * 1024 * 1024

TOKEN_TILE = 1024
POST_TILE = 256
FF_CHUNK = 1024
MLP_CHUNKS_PER_STAGE = (2, 1, 1)
ATTN_SUBTILES = 16
RELAYOUT_RADIX = 4


def _rms(x, g):
    ms = jnp.mean(x * x, axis=-1, keepdims=True)
    return x * jax.lax.rsqrt(ms + NORM_EPS) * g


def _dot(a, b):
    return jnp.dot(a, b, preferred_element_type=jnp.float32)


def _dot_nt(a, b):
    return jax.lax.dot_general(a, b, (((1,), (1,)), ((), ())),
                               preferred_element_type=jnp.float32)


def _params(n_axes):
    return pltpu.CompilerParams(
        dimension_semantics=("arbitrary",) * n_axes,
        vmem_limit_bytes=VMEM_LIMIT_BYTES)


def _resident(shape):
    return pl.BlockSpec(shape, lambda *_: (0,) * len(shape), pipeline_mode=pl.Buffered(1))


def _slab(j):
    return slice(j * LANES, (j + 1) * LANES)


def _split_streams(nat_ref, out4_ref, out16_ref, by4_ref):
    tm = nat_ref.shape[1]
    n4, n16 = tm // 4, tm // 16
    for r in range(RELAYOUT_RADIX):
        for j in range(N_SLABS):
            piece = nat_ref[j, pl.ds(r, n4, stride=RELAYOUT_RADIX), :]
            by4_ref[j, r * n4:(r + 1) * n4, :] = piece
            out4_ref[r, :, _slab(j)] = piece.astype(jnp.bfloat16)
    for r4 in range(RELAYOUT_RADIX):
        for a in range(RELAYOUT_RADIX):
            for j in range(N_SLABS):
                piece = by4_ref[j, pl.ds(r4 * n4 + a, n16, stride=RELAYOUT_RADIX), :]
                out16_ref[RELAYOUT_RADIX * a + r4, :, _slab(j)] = piece.astype(jnp.bfloat16)


def _merge_streams4(in4_ref, nat_ref):
    n4 = in4_ref.shape[1]
    for r in range(RELAYOUT_RADIX):
        for j in range(nat_ref.shape[0]):
            nat_ref[j, pl.ds(r, n4, stride=RELAYOUT_RADIX), :] = in4_ref[r, :, _slab(j)]


def _merge_streams16(in16_ref, by4_ref, nat_ref):
    n16 = in16_ref.shape[1]
    n4 = RELAYOUT_RADIX * n16
    n_slabs = nat_ref.shape[0]
    for r4 in range(RELAYOUT_RADIX):
        for a in range(RELAYOUT_RADIX):
            for j in range(n_slabs):
                by4_ref[j, pl.ds(r4 * n4 + a, n16, stride=RELAYOUT_RADIX), :] = (
                    in16_ref[RELAYOUT_RADIX * a + r4, :, _slab(j)])
    for r4 in range(RELAYOUT_RADIX):
        for j in range(n_slabs):
            nat_ref[j, pl.ds(r4, n4, stride=RELAYOUT_RADIX), :] = (
                by4_ref[j, r4 * n4:(r4 + 1) * n4, :])


def _in_proj_kernel(tiles_per_seq, x_ref, g_ref, w_ref, cw_ref, gc_ref,
                    q1_ref, k1_ref, v1_ref, q4_ref, k4_ref, v4_ref,
                    q16_ref, k16_ref, v16_ref, c_ref, ubuf_ref, nat_ref, by4_ref):
    tm = x_ref.shape[0]
    h = _rms(x_ref[...], g_ref[...]).astype(jnp.bfloat16)

    def proj(col):
        return _dot(h, w_ref[:, col * ATT_WIDTH:(col + 1) * ATT_WIDTH])

    def project_streams(col, out1_ref, out4_ref, out16_ref, scale=None):
        val = proj(col) if scale is None else proj(col) * scale
        out1_ref[...] = val.astype(jnp.bfloat16)
        for j in range(N_SLABS):
            nat_ref[j] = val[:, _slab(j)]
        _split_streams(nat_ref, out4_ref, out16_ref, by4_ref)

    project_streams(0, q1_ref, q4_ref, q16_ref, LOG2_E / math.sqrt(HEAD_DIM))
    project_streams(1, k1_ref, k4_ref, k16_ref)
    project_streams(2, v1_ref, v4_ref, v16_ref)

    @pl.when(pl.program_id(0) % tiles_per_seq == 0)
    def _():
        ubuf_ref[tm:tm + F32_SUBLANES, :] = jnp.zeros((F32_SUBLANES, CONV_WIDTH), jnp.float32)

    ubuf_ref[0:F32_SUBLANES, :] = ubuf_ref[tm:tm + F32_SUBLANES, :]
    u = proj(4) * proj(5)
    ubuf_ref[F32_SUBLANES:tm + F32_SUBLANES, :] = u
    conv = (u * cw_ref[2:3, :]
            + ubuf_ref[F32_SUBLANES - 1:tm + F32_SUBLANES - 1, :] * cw_ref[1:2, :]
            + ubuf_ref[F32_SUBLANES - 2:tm + F32_SUBLANES - 2, :] * cw_ref[0:1, :])
    y = proj(3) * conv
    c_ref[...] = _rms(y, gc_ref[...]).astype(jnp.bfloat16)


def _stream_spec(dilation, tm, tiles_per_seq):
    return pl.BlockSpec((None, dilation, tm // dilation, ATT_WIDTH),
                        lambda i: (i // tiles_per_seq, 0, i % tiles_per_seq, 0))


def _in_proj(x2d, g_mix, w_in, conv_w, g_conv_out, batch, seq_len):
    t = x2d.shape[0]
    tm = TOKEN_TILE
    assert seq_len % tm == 0 and t == batch * seq_len
    tiles_per_seq = seq_len // tm
    nat = jax.ShapeDtypeStruct((t, ATT_WIDTH), jnp.bfloat16)
    by = lambda d: jax.ShapeDtypeStruct((batch, d, seq_len // d, ATT_WIDTH), jnp.bfloat16)
    row_spec = pl.BlockSpec((tm, ATT_WIDTH), lambda i: (i, 0))
    s4, s16 = _stream_spec(4, tm, tiles_per_seq), _stream_spec(16, tm, tiles_per_seq)
    return pl.pallas_call(
        functools.partial(_in_proj_kernel, tiles_per_seq),
        grid=(t // tm,),
        in_specs=[
            pl.BlockSpec((tm, D_MODEL), lambda i: (i, 0)),
            _resident((1, D_MODEL)),
            _resident(w_in.shape),
            _resident(conv_w.shape),
            _resident((1, CONV_WIDTH)),
        ],
        out_specs=[row_spec] * 3 + [s4] * 3 + [s16] * 3 + [row_spec],
        out_shape=[nat] * 3 + [by(4)] * 3 + [by(16)] * 3 + [nat],
        scratch_shapes=[pltpu.VMEM((tm + F32_SUBLANES, CONV_WIDTH), jnp.float32),
                        pltpu.VMEM((N_SLABS, tm, LANES), jnp.float32),
                        pltpu.VMEM((N_SLABS, tm, LANES), jnp.float32)],
        compiler_params=_params(1),
        name="in_proj",
    )(x2d, g_mix.reshape(1, -1), w_in, conv_w, g_conv_out.reshape(1, -1))


def _attn_kernel(q_ref, kc_ref, kp_ref, vc_ref, vp_ref, o_ref, lse_ref):
    s = BAND_STEPS
    streams, row_blocks = q_ref.shape[0], q_ref.shape[1] // s
    has_prev = pl.program_id(2) > 0
    qi = jax.lax.broadcasted_iota(jnp.int32, (2 * s, 2 * s), 0) & (s - 1)
    kj = jax.lax.broadcasted_iota(jnp.int32, (2 * s, 2 * s), 1)
    band = (kj >= qi) & (kj <= qi + s)
    cap = jnp.where(band, jnp.inf, NEG_INF)
    cap_first = jnp.where(band & ((kj >= s) | has_prev), jnp.inf, NEG_INF)
    first_head = jax.lax.broadcasted_iota(jnp.int32, (s, PAIR_LANES), 1) < HEAD_DIM
    ones = jnp.ones((2 * s, PAIR_LANES), jnp.bfloat16)

    def rows_of(i):
        return slice(i * s, (i + 1) * s)

    def scores(r, i, pair):
        lanes = _slab(pair)
        q2 = q_ref[r, rows_of(i), lanes]
        zero = jnp.zeros_like(q2)
        q = jnp.concatenate([jnp.where(first_head, q2, zero),
                             jnp.where(first_head, zero, q2)], axis=0)
        k_prev = kp_ref[r, :, lanes] if i == 0 else kc_ref[r, rows_of(i - 1), lanes]
        k = jnp.concatenate([k_prev, kc_ref[r, rows_of(i), lanes]], axis=0)
        return jnp.minimum(_dot_nt(q, k), cap_first if i == 0 else cap)

    def finish(r, i, pair, sc):
        lanes = _slab(pair)
        v_prev = vp_ref[r, :, lanes] if i == 0 else vc_ref[r, rows_of(i - 1), lanes]
        v = jnp.concatenate(
            [jnp.concatenate([v_prev, vc_ref[r, rows_of(i), lanes]], axis=0), ones], axis=1)
        m = jnp.max(sc, axis=-1, keepdims=True)
        p = jnp.exp2(sc - m)
        pv = _dot(p.astype(jnp.bfloat16), v)
        l = pv[:, PAIR_LANES:]
        o = pv[:, :PAIR_LANES] / l
        o_ref[r, rows_of(i), lanes] = jnp.where(first_head, o[:s], o[s:])
        lse = m + jnp.log2(l)
        for half in range(2):
            at = slice((2 * pair + half) * LSE_REP, (2 * pair + half + 1) * LSE_REP)
            lse_ref[r, rows_of(i), at] = lse[half * s:(half + 1) * s, at]

    items = [(r, i, pair) for r in range(streams) for i in range(row_blocks)
             for pair in range(N_ATT_HEADS // 2)]
    sc = scores(*items[0])
    for n, item in enumerate(items):
        sc_next = scores(*items[n + 1]) if n + 1 < len(items) else None
        finish(*item, sc)
        sc = sc_next


def _dilated_attention(q, k, v):
    b, d, steps, _ = q.shape
    row_blocks = min(ATTN_SUBTILES, steps // BAND_STEPS)
    streams = ATTN_SUBTILES // row_blocks
    tile_rows = row_blocks * BAND_STEPS
    assert steps % tile_rows == 0 and d % streams == 0
    cur = pl.BlockSpec((None, streams, tile_rows, ATT_WIDTH), lambda bi, g, n: (bi, g, n, 0))
    prev = pl.BlockSpec((None, streams, BAND_STEPS, ATT_WIDTH),
                        lambda bi, g, n: (bi, g, jnp.maximum(n * row_blocks - 1, 0), 0))
    cur_lse = pl.BlockSpec((None, streams, tile_rows, LANES), lambda bi, g, n: (bi, g, n, 0))
    return pl.pallas_call(
        _attn_kernel,
        grid=(b, d // streams, steps // tile_rows),
        in_specs=[cur, cur, prev, cur, prev],
        out_specs=[cur, cur_lse],
        out_shape=[jax.ShapeDtypeStruct(q.shape, jnp.float32),
                   jax.ShapeDtypeStruct((b, d, steps, LANES), jnp.float32)],
        compiler_params=_params(3),
        name=f"dilated_attn_d{d}",
    )(q, k, k, v, v)


def _post_attn_kernel(o1_ref, l1_ref, o4_ref, l4_ref, o16_ref, l16_ref, c_ref, x_ref,
                      ga_ref, wout_ref, gx_ref, wq_ref, kv_ref, wo_ref,
                      gm_ref, wu_ref, wd_ref, gf_ref, y_ref,
                      on4_ref, ln4_ref, on16_ref, ln16_ref, by4_ref, xmid_ref):
    i = pl.program_id(0)

    @pl.when(i == 0)
    def _():
        xmid_ref[...] = jnp.zeros(xmid_ref.shape, jnp.float32)

    assert sum(MLP_CHUNKS_PER_STAGE) * FF_CHUNK == D_FF
    y = xmid_ref[(i + 1) % 2]
    h_mlp = _rms(y, gm_ref[...]).astype(jnp.bfloat16)
    chunk_ids = iter(range(D_FF // FF_CHUNK))

    def mlp_stage(y, stage):
        for _ in range(MLP_CHUNKS_PER_STAGE[stage]):
            c = next(chunk_ids)
            cols = slice(c * FF_CHUNK, (c + 1) * FF_CHUNK)
            a = jnp.maximum(_dot(h_mlp, wu_ref[:, cols]), 0.0)
            y = y + _dot((a * a).astype(jnp.bfloat16), wd_ref[cols, :])
        return y

    _merge_streams4(o4_ref, on4_ref)
    _merge_streams4(l4_ref, ln4_ref)
    _merge_streams16(o16_ref, by4_ref, on16_ref)
    _merge_streams16(l16_ref, by4_ref, ln16_ref)
    lses = [l1_ref[...], ln4_ref[0], ln16_ref[0]]
    top = jnp.maximum(jnp.maximum(lses[0], lses[1]), lses[2])
    es = [jnp.exp2(l - top) for l in lses]
    inv = 1.0 / (es[0] + es[1] + es[2])
    weights = [e * inv for e in es]
    lane = jax.lax.broadcasted_iota(jnp.int32, weights[0].shape, 1)
    slabs = []
    for j in range(N_SLABS):
        src = (2 * j + lane // HEAD_DIM) * LSE_REP
        w = [jnp.take_along_axis(wp, src, axis=1) for wp in weights]
        slabs.append(w[0] * o1_ref[:, _slab(j)] + w[1] * on4_ref[j] + w[2] * on16_ref[j])
    attn = jnp.concatenate(slabs, axis=-1)
    attn_n = _rms(attn, ga_ref[...]).astype(jnp.bfloat16)
    y = mlp_stage(y, 0)
    x = (x_ref[...] + _dot(attn_n, wout_ref[0:ATT_WIDTH, :])
         + _dot(c_ref[...], wout_ref[ATT_WIDTH:, :]))

    y = mlp_stage(y, 1)

    h = _rms(x, gx_ref[...]).astype(jnp.bfloat16)
    q = (_dot(h, wq_ref[...]) * (LOG2_E / math.sqrt(MEM_HEAD_DIM))).astype(jnp.bfloat16)
    head = lambda hd: slice(hd * MEM_HEAD_DIM, (hd + 1) * MEM_HEAD_DIM)
    scores = [_dot_nt(q[:, head(hd)], kv_ref[:, head(hd)]) for hd in range(N_MEM_HEADS)]

    y = mlp_stage(y, 2)
    y_ref[...] = _rms(y, gf_ref[...])

    heads = []
    for hd in range(N_MEM_HEADS):
        sc = scores[hd]
        m = jnp.max(sc, axis=-1, keepdims=True)
        p = jnp.exp2(sc - m)
        l = jnp.sum(p, axis=-1, keepdims=True)
        v = kv_ref[:, D_MODEL + hd * MEM_HEAD_DIM:D_MODEL + (hd + 1) * MEM_HEAD_DIM]
        heads.append((_dot(p.astype(jnp.bfloat16), v) / l).astype(jnp.bfloat16))
    xmid_ref[i % 2] = x + _dot(jnp.concatenate(heads, axis=-1), wo_ref[...])


def _post_attn(o1, l1, o4, l4, o16, l16, convn, x2d, kv, g_attn_out, w_out, g_xattn, w_q, w_o,
               g_mlp, w_up, w_down, g_final, seq_len, mem_len):
    t = x2d.shape[0]
    tm = POST_TILE
    assert seq_len % tm == 0
    tiles_per_seq = seq_len // tm
    n_tiles = t // tm
    tile_in = lambda i: jnp.minimum(i, n_tiles - 1)
    row_spec = lambda width: pl.BlockSpec((tm, width), lambda i: (tile_in(i), 0))
    stream_spec = lambda d, width: pl.BlockSpec(
        (None, d, tm // d, width),
        lambda i: (tile_in(i) // tiles_per_seq, 0, tile_in(i) % tiles_per_seq, 0))
    x_spec = pl.BlockSpec((tm, D_MODEL), lambda i: (tile_in(i), 0))
    gain = lambda g: g.reshape(1, -1)
    nat = pltpu.VMEM((N_SLABS, tm, LANES), jnp.float32)
    nat_lse = pltpu.VMEM((1, tm, LANES), jnp.float32)
    return pl.pallas_call(
        _post_attn_kernel,
        grid=(n_tiles + 1,),
        in_specs=[row_spec(ATT_WIDTH), row_spec(LANES),
                  stream_spec(4, ATT_WIDTH), stream_spec(4, LANES),
                  stream_spec(16, ATT_WIDTH), stream_spec(16, LANES),
                  row_spec(ATT_WIDTH), x_spec,
                  _resident((1, ATT_WIDTH)), _resident(w_out.shape),
                  _resident((1, D_MODEL)), _resident(w_q.shape),
                  pl.BlockSpec((mem_len, 2 * D_MODEL),
                               lambda i: (tile_in(i) // tiles_per_seq, 0)),
                  _resident(w_o.shape),
                  _resident((1, D_MODEL)), _resident(w_up.shape), _resident(w_down.shape),
                  _resident((1, D_MODEL))],
        out_specs=pl.BlockSpec((tm, D_MODEL), lambda i: (jnp.maximum(i - 1, 0), 0)),
        out_shape=jax.ShapeDtypeStruct((t, D_MODEL), jnp.float32),
        scratch_shapes=[nat, nat_lse, nat, nat_lse, nat,
                        pltpu.VMEM((2, tm, D_MODEL), jnp.float32)],
        compiler_params=_params(1),
        name="post_attn",
    )(o1, l1, o4, l4, o16, l16, convn, x2d, gain(g_attn_out), w_out, gain(g_xattn), w_q, kv, w_o,
      gain(g_mlp), w_up, w_down, gain(g_final))


def _mem_kv_kernel(m_ref, g_ref, w_ref, kv_ref):
    h = _rms(m_ref[...], g_ref[...]).astype(jnp.bfloat16)
    kv_ref[...] = _dot(h, w_ref[...]).astype(jnp.bfloat16)


def _mem_kv(mem2d, g_mem, w_kv):
    rows = mem2d.shape[0]
    tm = TOKEN_TILE
    return pl.pallas_call(
        _mem_kv_kernel,
        grid=(rows // tm,),
        in_specs=[pl.BlockSpec((tm, D_MODEL), lambda i: (i, 0)),
                  _resident((1, D_MODEL)), _resident(w_kv.shape)],
        out_specs=pl.BlockSpec((tm, 2 * D_MODEL), lambda i: (i, 0)),
        out_shape=jax.ShapeDtypeStruct((rows, 2 * D_MODEL), jnp.bfloat16),
        compiler_params=_params(1),
        name="mem_kv_proj",
    )(mem2d, g_mem.reshape(1, -1), w_kv)


def kernel(x, mem, g_mix, w_in, conv_w, g_attn_out, g_conv_out, w_out, g_xattn, g_mem,
           w_q_mem, w_kv_mem, w_o_mem, g_mlp, w_up, w_down, g_final):
    b, seq, d = x.shape
    mem_len = mem.shape[1]
    assert tuple(dil for _, dil in DILATED_PATTERNS) == (1, 4, 16)
    assert all(win // dil == BAND_STEPS for win, dil in DILATED_PATTERNS)
    bf = lambda w: w.astype(jnp.bfloat16)
    x2d = x.reshape(b * seq, d)

    (q1, k1, v1, q4, k4, v4, q16, k16, v16, convn) = _in_proj(
        x2d, g_mix, bf(w_in), conv_w, g_conv_out, b, seq)
    as_stream = lambda a: a.reshape(b, 1, seq, ATT_WIDTH)
    o1, l1 = _dilated_attention(as_stream(q1), as_stream(k1), as_stream(v1))
    o4, l4 = _dilated_attention(q4, k4, v4)
    o16, l16 = _dilated_attention(q16, k16, v16)
    flat = lambda a: a.reshape(b * seq, a.shape[-1])
    kv = _mem_kv(mem.reshape(b * mem_len, d), g_mem, bf(w_kv_mem))
    y = _post_attn(flat(o1), flat(l1), o4, l4, o16, l16, convn, x2d, kv,
                   g_attn_out, bf(w_out), g_xattn, bf(w_q_mem), bf(w_o_mem),
                   g_mlp, bf(w_up), bf(w_down), g_final, seq, mem_len)
    return y.reshape(b, seq, d)
```

```python
import functools
import math

import jax
import jax.numpy as jnp
from jax.experimental import pallas as pl
from jax.experimental.pallas import tpu as pltpu

D_MODEL = 1024
ATT_WIDTH = 512
HEAD_DIM = 64
N_ATT_HEADS = 8
CONV_WIDTH = 512
CONV_K = 3
DILATED_PATTERNS = ((128, 1), (512, 4), (2048, 16))
BAND_STEPS = 128
N_MEM_HEADS = 4
MEM_HEAD_DIM = 256
D_FF = 4096
NORM_EPS = 1e-6
NEG_INF = -1e30
LOG2_E = math.log2(math.e)

LANES = 128
F32_SUBLANES = 8
PAIR_LANES = 2 * HEAD_DIM
assert PAIR_LANES == LANES
N_SLABS = ATT_WIDTH // LANES
LSE_REP = LANES // N_ATT_HEADS
VMEM_LIMIT_BYTES = 58 * 1024 * 1024

TOKEN_TILE = 1024
POST_TILE = 512
FF_CHUNK = 1024
MLP_CHUNKS_PER_STAGE = (2, 1, 1)
ATTN_SUBTILES = 16
RELAYOUT_RADIX = 4


def _rms(x, g):
    ms = jnp.mean(x * x, axis=-1, keepdims=True)
    return x * jax.lax.rsqrt(ms + NORM_EPS) * g


def _dot(a, b):
    return jnp.dot(a, b, preferred_element_type=jnp.float32)


def _dot_nt(a, b):
    return jax.lax.dot_general(a, b, (((1,), (1,)), ((), ())),
                               preferred_element_type=jnp.float32)


def _params(n_axes):
    return pltpu.CompilerParams(
        dimension_semantics=("arbitrary",) * n_axes,
        vmem_limit_bytes=VMEM_LIMIT_BYTES)


def _resident(shape):
    return pl.BlockSpec(shape, lambda *_: (0,) * len(shape), pipeline_mode=pl.Buffered(1))


def _slab(j):
    return slice(j * LANES, (j + 1) * LANES)


def _split_streams(nat_ref, out4_ref, out16_ref, by4_ref):
    tm = nat_ref.shape[1]
    n4, n16 = tm // 4, tm // 16
    for r in range(RELAYOUT_RADIX):
        for j in range(N_SLABS):
            piece = nat_ref[j, pl.ds(r, n4, stride=RELAYOUT_RADIX), :]
            by4_ref[j, r * n4:(r + 1) * n4, :] = piece
            out4_ref[r, :, _slab(j)] = piece.astype(jnp.bfloat16)
    for r4 in range(RELAYOUT_RADIX):
        for a in range(RELAYOUT_RADIX):
            for j in range(N_SLABS):
                piece = by4_ref[j, pl.ds(r4 * n4 + a, n16, stride=RELAYOUT_RADIX), :]
                out16_ref[RELAYOUT_RADIX * a + r4, :, _slab(j)] = piece.astype(jnp.bfloat16)


def _merge_streams4(in4_ref, nat_ref):
    n4 = in4_ref.shape[1]
    for r in range(RELAYOUT_RADIX):
        for j in range(nat_ref.shape[0]):
            nat_ref[j, pl.ds(r, n4, stride=RELAYOUT_RADIX), :] = in4_ref[r, :, _slab(j)]


def _merge_streams16(in16_ref, by4_ref, nat_ref):
    n16 = in16_ref.shape[1]
    n4 = RELAYOUT_RADIX * n16
    n_slabs = nat_ref.shape[0]
    for r4 in range(RELAYOUT_RADIX):
        for a in range(RELAYOUT_RADIX):
            for j in range(n_slabs):
                by4_ref[j, pl.ds(r4 * n4 + a, n16, stride=RELAYOUT_RADIX), :] = (
                    in16_ref[RELAYOUT_RADIX * a + r4, :, _slab(j)])
    for r4 in range(RELAYOUT_RADIX):
        for j in range(n_slabs):
            nat_ref[j, pl.ds(r4, n4, stride=RELAYOUT_RADIX), :] = (
                by4_ref[j, r4 * n4:(r4 + 1) * n4, :])


def _in_proj_kernel(tiles_per_seq, x_ref, g_ref, w_ref, cw_ref, gc_ref,
                    q1_ref, k1_ref, v1_ref, q4_ref, k4_ref, v4_ref,
                    q16_ref, k16_ref, v16_ref, c_ref, ubuf_ref, nat_ref, by4_ref):
    tm = x_ref.shape[0]
    h = _rms(x_ref[...], g_ref[...]).astype(jnp.bfloat16)

    def proj(col):
        return _dot(h, w_ref[:, col * ATT_WIDTH:(col + 1) * ATT_WIDTH])

    def project_streams(col, out1_ref, out4_ref, out16_ref, scale=None):
        val = proj(col) if scale is None else proj(col) * scale
        out1_ref[...] = val.astype(jnp.bfloat16)
        for j in range(N_SLABS):
            nat_ref[j] = val[:, _slab(j)]
        _split_streams(nat_ref, out4_ref, out16_ref, by4_ref)

    project_streams(0, q1_ref, q4_ref, q16_ref, LOG2_E / math.sqrt(HEAD_DIM))
    project_streams(1, k1_ref, k4_ref, k16_ref)
    project_streams(2, v1_ref, v4_ref, v16_ref)

    @pl.when(pl.program_id(0) % tiles_per_seq == 0)
    def _():
        ubuf_ref[tm:tm + F32_SUBLANES, :] = jnp.zeros((F32_SUBLANES, CONV_WIDTH), jnp.float32)

    ubuf_ref[0:F32_SUBLANES, :] = ubuf_ref[tm:tm + F32_SUBLANES, :]
    u = proj(4) * proj(5)
    ubuf_ref[F32_SUBLANES:tm + F32_SUBLANES, :] = u
    conv = (u * cw_ref[2:3, :]
            + ubuf_ref[F32_SUBLANES - 1:tm + F32_SUBLANES - 1, :] * cw_ref[1:2, :]
            + ubuf_ref[F32_SUBLANES - 2:tm + F32_SUBLANES - 2, :] * cw_ref[0:1, :])
    y = proj(3) * conv
    c_ref[...] = _rms(y, gc_ref[...]).astype(jnp.bfloat16)


def _stream_spec(dilation, tm, tiles_per_seq):
    return pl.BlockSpec((None, dilation, tm // dilation, ATT_WIDTH),
                        lambda i: (i // tiles_per_seq, 0, i % tiles_per_seq, 0))


def _in_proj(x2d, g_mix, w_in, conv_w, g_conv_out, batch, seq_len):
    t = x2d.shape[0]
    tm = TOKEN_TILE
    assert seq_len % tm == 0 and t == batch * seq_len
    tiles_per_seq = seq_len // tm
    nat = jax.ShapeDtypeStruct((t, ATT_WIDTH), jnp.bfloat16)
    by = lambda d: jax.ShapeDtypeStruct((batch, d, seq_len // d, ATT_WIDTH), jnp.bfloat16)
    row_spec = pl.BlockSpec((tm, ATT_WIDTH), lambda i: (i, 0))
    s4, s16 = _stream_spec(4, tm, tiles_per_seq), _stream_spec(16, tm, tiles_per_seq)
    return pl.pallas_call(
        functools.partial(_in_proj_kernel, tiles_per_seq),
        grid=(t // tm,),
        in_specs=[
            pl.BlockSpec((tm, D_MODEL), lambda i: (i, 0)),
            _resident((1, D_MODEL)),
            _resident(w_in.shape),
            _resident(conv_w.shape),
            _resident((1, CONV_WIDTH)),
        ],
        out_specs=[row_spec] * 3 + [s4] * 3 + [s16] * 3 + [row_spec],
        out_shape=[nat] * 3 + [by(4)] * 3 + [by(16)] * 3 + [nat],
        scratch_shapes=[pltpu.VMEM((tm + F32_SUBLANES, CONV_WIDTH), jnp.float32),
                        pltpu.VMEM((N_SLABS, tm, LANES), jnp.float32),
                        pltpu.VMEM((N_SLABS, tm, LANES), jnp.float32)],
        compiler_params=_params(1),
        name="in_proj",
    )(x2d, g_mix.reshape(1, -1), w_in, conv_w, g_conv_out.reshape(1, -1))


def _attn_kernel(q_ref, kc_ref, kp_ref, vc_ref, vp_ref, o_ref, lse_ref):
    s = BAND_STEPS
    streams, row_blocks = q_ref.shape[0], q_ref.shape[1] // s
    has_prev = pl.program_id(2) > 0
    qi = jax.lax.broadcasted_iota(jnp.int32, (2 * s, 2 * s), 0) & (s - 1)
    kj = jax.lax.broadcasted_iota(jnp.int32, (2 * s, 2 * s), 1)
    band = (kj >= qi) & (kj <= qi + s)
    cap = jnp.where(band, jnp.inf, NEG_INF)
    cap_first = jnp.where(band & ((kj >= s) | has_prev), jnp.inf, NEG_INF)
    first_head = jax.lax.broadcasted_iota(jnp.int32, (s, PAIR_LANES), 1) < HEAD_DIM
    ones = jnp.ones((2 * s, PAIR_LANES), jnp.bfloat16)

    def rows_of(i):
        return slice(i * s, (i + 1) * s)

    def scores(r, i, pair):
        lanes = _slab(pair)
        q2 = q_ref[r, rows_of(i), lanes]
        zero = jnp.zeros_like(q2)
        q = jnp.concatenate([jnp.where(first_head, q2, zero),
                             jnp.where(first_head, zero, q2)], axis=0)
        k_prev = kp_ref[r, :, lanes] if i == 0 else kc_ref[r, rows_of(i - 1), lanes]
        k = jnp.concatenate([k_prev, kc_ref[r, rows_of(i), lanes]], axis=0)
        return jnp.minimum(_dot_nt(q, k), cap_first if i == 0 else cap)

    def finish(r, i, pair, sc):
        lanes = _slab(pair)
        v_prev = vp_ref[r, :, lanes] if i == 0 else vc_ref[r, rows_of(i - 1), lanes]
        v = jnp.concatenate(
            [jnp.concatenate([v_prev, vc_ref[r, rows_of(i), lanes]], axis=0), ones], axis=1)
        m = jnp.max(sc, axis=-1, keepdims=True)
        p = jnp.exp2(sc - m)
        pv = _dot(p.astype(jnp.bfloat16), v)
        l = pv[:, PAIR_LANES:]
        o = pv[:, :PAIR_LANES] / l
        o_ref[r, rows_of(i), lanes] = jnp.where(first_head, o[:s], o[s:])
        lse = m + jnp.log2(l)
        for half in range(2):
            at = slice((2 * pair + half) * LSE_REP, (2 * pair + half + 1) * LSE_REP)
            lse_ref[r, rows_of(i), at] = lse[half * s:(half + 1) * s, at]

    items = [(r, i, pair) for r in range(streams) for i in range(row_blocks)
             for pair in range(N_ATT_HEADS // 2)]
    sc = scores(*items[0])
    for n, item in enumerate(items):
        sc_next = scores(*items[n + 1]) if n + 1 < len(items) else None
        finish(*item, sc)
        sc = sc_next


def _dilated_attention(q, k, v):
    b, d, steps, _ = q.shape
    row_blocks = min(ATTN_SUBTILES, steps // BAND_STEPS)
    streams = ATTN_SUBTILES // row_blocks
    tile_rows = row_blocks * BAND_STEPS
    assert steps % tile_rows == 0 and d % streams == 0
    cur = pl.BlockSpec((None, streams, tile_rows, ATT_WIDTH), lambda bi, g, n: (bi, g, n, 0))
    prev = pl.BlockSpec((None, streams, BAND_STEPS, ATT_WIDTH),
                        lambda bi, g, n: (bi, g, jnp.maximum(n * row_blocks - 1, 0), 0))
    cur_lse = pl.BlockSpec((None, streams, tile_rows, LANES), lambda bi, g, n: (bi, g, n, 0))
    return pl.pallas_call(
        _attn_kernel,
        grid=(b, d // streams, steps // tile_rows),
        in_specs=[cur, cur, prev, cur, prev],
        out_specs=[cur, cur_lse],
        out_shape=[jax.ShapeDtypeStruct(q.shape, jnp.float32),
                   jax.ShapeDtypeStruct((b, d, steps, LANES), jnp.float32)],
        compiler_params=_params(3),
        name=f"dilated_attn_d{d}",
    )(q, k, k, v, v)


def _post_attn_kernel(o1_ref, l1_ref, o4_ref, l4_ref, o16_ref, l16_ref, c_ref, x_ref,
                      ga_ref, wout_ref, gx_ref, wq_ref, kv_ref, wo_ref,
                      gm_ref, wu_ref, wd_ref, gf_ref, y_ref,
                      on4_ref, ln4_ref, on16_ref, ln16_ref, by4_ref, xmid_ref):
    i = pl.program_id(0)

    @pl.when(i == 0)
    def _():
        xmid_ref[...] = jnp.zeros(xmid_ref.shape, jnp.float32)

    assert sum(MLP_CHUNKS_PER_STAGE) * FF_CHUNK == D_FF
    y = xmid_ref[(i + 1) % 2]
    h_mlp = _rms(y, gm_ref[...]).astype(jnp.bfloat16)
    chunk_ids = iter(range(D_FF // FF_CHUNK))

    def mlp_stage(y, stage):
        for _ in range(MLP_CHUNKS_PER_STAGE[stage]):
            c = next(chunk_ids)
            cols = slice(c * FF_CHUNK, (c + 1) * FF_CHUNK)
            a = jnp.maximum(_dot(h_mlp, wu_ref[:, cols]), 0.0)
            y = y + _dot((a * a).astype(jnp.bfloat16), wd_ref[cols, :])
        return y

    _merge_streams4(o4_ref, on4_ref)
    _merge_streams4(l4_ref, ln4_ref)
    _merge_streams16(o16_ref, by4_ref, on16_ref)
    _merge_streams16(l16_ref, by4_ref, ln16_ref)
    lses = [l1_ref[...], ln4_ref[0], ln16_ref[0]]
    top = jnp.maximum(jnp.maximum(lses[0], lses[1]), lses[2])
    es = [jnp.exp2(l - top) for l in lses]
    inv = 1.0 / (es[0] + es[1] + es[2])
    weights = [e * inv for e in es]
    lane = jax.lax.broadcasted_iota(jnp.int32, weights[0].shape, 1)
    slabs = []
    for j in range(N_SLABS):
        src = (2 * j + lane // HEAD_DIM) * LSE_REP
        w = [jnp.take_along_axis(wp, src, axis=1) for wp in weights]
        slabs.append(w[0] * o1_ref[:, _slab(j)] + w[1] * on4_ref[j] + w[2] * on16_ref[j])
    attn = jnp.concatenate(slabs, axis=-1)
    attn_n = _rms(attn, ga_ref[...]).astype(jnp.bfloat16)
    y = mlp_stage(y, 0)
    x = (x_ref[...] + _dot(attn_n, wout_ref[0:ATT_WIDTH, :])
         + _dot(c_ref[...], wout_ref[ATT_WIDTH:, :]))

    y = mlp_stage(y, 1)

    h = _rms(x, gx_ref[...]).astype(jnp.bfloat16)
    q = (_dot(h, wq_ref[...]) * (LOG2_E / math.sqrt(MEM_HEAD_DIM))).astype(jnp.bfloat16)
    head = lambda hd: slice(hd * MEM_HEAD_DIM, (hd + 1) * MEM_HEAD_DIM)
    scores = [_dot_nt(q[:, head(hd)], kv_ref[:, head(hd)]) for hd in range(N_MEM_HEADS)]

    y = mlp_stage(y, 2)
    y_ref[...] = _rms(y, gf_ref[...])

    heads = []
    for hd in range(N_MEM_HEADS):
        sc = scores[hd]
        m = jnp.max(sc, axis=-1, keepdims=True)
        p = jnp.exp2(sc - m)
        l = jnp.sum(p, axis=-1, keepdims=True)
        v = kv_ref[:, D_MODEL + hd * MEM_HEAD_DIM:D_MODEL + (hd + 1) * MEM_HEAD_DIM]
        heads.append((_dot(p.astype(jnp.bfloat16), v) / l).astype(jnp.bfloat16))
    xmid_ref[i % 2] = x + _dot(jnp.concatenate(heads, axis=-1), wo_ref[...])


def _post_attn(o1, l1, o4, l4, o16, l16, convn, x2d, kv, g_attn_out, w_out, g_xattn, w_q, w_o,
               g_mlp, w_up, w_down, g_final, seq_len, mem_len):
    t = x2d.shape[0]
    tm = POST_TILE
    assert seq_len % tm == 0
    tiles_per_seq = seq_len // tm
    n_tiles = t // tm
    tile_in = lambda i: jnp.minimum(i, n_tiles - 1)
    row_spec = lambda width: pl.BlockSpec((tm, width), lambda i: (tile_in(i), 0))
    stream_spec = lambda d, width: pl.BlockSpec(
        (None, d, tm // d, width),
        lambda i: (tile_in(i) // tiles_per_seq, 0, tile_in(i) % tiles_per_seq, 0))
    x_spec = pl.BlockSpec((tm, D_MODEL), lambda i: (tile_in(i), 0))
    gain = lambda g: g.reshape(1, -1)
    nat = pltpu.VMEM((N_SLABS, tm, LANES), jnp.float32)
    nat_lse = pltpu.VMEM((1, tm, LANES), jnp.float32)
    return pl.pallas_call(
        _post_attn_kernel,
        grid=(n_tiles + 1,),
        in_specs=[row_spec(ATT_WIDTH), row_spec(LANES),
                  stream_spec(4, ATT_WIDTH), stream_spec(4, LANES),
                  stream_spec(16, ATT_WIDTH), stream_spec(16, LANES),
                  row_spec(ATT_WIDTH), x_spec,
                  _resident((1, ATT_WIDTH)), _resident(w_out.shape),
                  _resident((1, D_MODEL)), _resident(w_q.shape),
                  pl.BlockSpec((mem_len, 2 * D_MODEL),
                               lambda i: (tile_in(i) // tiles_per_seq, 0)),
                  _resident(w_o.shape),
                  _resident((1, D_MODEL)), _resident(w_up.shape), _resident(w_down.shape),
                  _resident((1, D_MODEL))],
        out_specs=pl.BlockSpec((tm, D_MODEL), lambda i: (jnp.maximum(i - 1, 0), 0)),
        out_shape=jax.ShapeDtypeStruct((t, D_MODEL), jnp.float32),
        scratch_shapes=[nat, nat_lse, nat, nat_lse, nat,
                        pltpu.VMEM((2, tm, D_MODEL), jnp.float32)],
        compiler_params=_params(1),
        name="post_attn",
    )(o1, l1, o4, l4, o16, l16, convn, x2d, gain(g_attn_out), w_out, gain(g_xattn), w_q, kv, w_o,
      gain(g_mlp), w_up, w_down, gain(g_final))


def _mem_kv_kernel(m_ref, g_ref, w_ref, kv_ref):
    h = _rms(m_ref[...], g_ref[...]).astype(jnp.bfloat16)
    kv_ref[...] = _dot(h, w_ref[...]).astype(jnp.bfloat16)


def _mem_kv(mem2d, g_mem, w_kv):
    rows = mem2d.shape[0]
    tm = TOKEN_TILE
    return pl.pallas_call(
        _mem_kv_kernel,
        grid=(rows // tm,),
        in_specs=[pl.BlockSpec((tm, D_MODEL), lambda i: (i, 0)),
                  _resident((1, D_MODEL)), _resident(w_kv.shape)],
        out_specs=pl.BlockSpec((tm, 2 * D_MODEL), lambda i: (i, 0)),
        out_shape=jax.ShapeDtypeStruct((rows, 2 * D_MODEL), jnp.bfloat16),
        compiler_params=_params(1),
        name="mem_kv_proj",
    )(mem2d, g_mem.reshape(1, -1), w_kv)


def kernel(x, mem, g_mix, w_in, conv_w, g_attn_out, g_conv_out, w_out, g_xattn, g_mem,
           w_q_mem, w_kv_mem, w_o_mem, g_mlp, w_up, w_down, g_final):
    b, seq, d = x.shape
    mem_len = mem.shape[1]
    assert tuple(dil for _, dil in DILATED_PATTERNS) == (1, 4, 16)
    assert all(win // dil == BAND_STEPS for win, dil in DILATED_PATTERNS)
    bf = lambda w: w.astype(jnp.bfloat16)
    x2d = x.reshape(b * seq, d)

    (q1, k1, v1, q4, k4, v4, q16, k16, v16, convn) = _in_proj(
        x2d, g_mix, bf(w_in), conv_w, g_conv_out, b, seq)
    as_stream = lambda a: a.reshape(b, 1, seq, ATT_WIDTH)
    o1, l1 = _dilated_attention(as_stream(q1), as_stream(k1), as_stream(v1))
    o4, l4 = _dilated_attention(q4, k4, v4)
    o16, l16 = _dilated_attention(q16, k16, v16)
    flat = lambda a: a.reshape(b * seq, a.shape[-1])
    kv = _mem_kv(mem.reshape(b * mem_len, d), g_mem, bf(w_kv_mem))
    y = _post_attn(flat(o1), flat(l1), o4, l4, o16, l16, convn, x2d, kv,
                   g_attn_out, bf(w_out), g_xattn, bf(w_q_mem), bf(w_o_mem),
                   g_mlp, bf(w_up), bf(w_down), g_final, seq, mem_len)
    return y.reshape(b, seq, d)
```

```python
import functools
import math

import jax
import jax.numpy as jnp
from jax.experimental import pallas as pl
from jax.experimental.pallas import tpu as pltpu

D_MODEL = 1024
ATT_WIDTH = 512
HEAD_DIM = 64
N_ATT_HEADS = 8
CONV_WIDTH = 512
CONV_K = 3
DILATED_PATTERNS = ((128, 1), (512, 4), (2048, 16))
BAND_STEPS = 128
N_MEM_HEADS = 4
MEM_HEAD_DIM = 256
D_FF = 4096
NORM_EPS = 1e-6
NEG_INF = -1e30
LOG2_E = math.log2(math.e)

LANES = 128
F32_SUBLANES = 8
PAIR_LANES = 2 * HEAD_DIM
assert PAIR_LANES == LANES
N_SLABS = ATT_WIDTH // LANES
LSE_REP = LANES // N_ATT_HEADS
VMEM_LIMIT_BYTES = 58 * 1024 * 1024

TOKEN_TILE = 1024
POST_TILE = 512
FF_CHUNK = 1024
MLP_CHUNKS_PER_STAGE = (2, 1, 1)
ATTN_SUBTILES = 16
RELAYOUT_RADIX = 4


def _rms(x, g):
    ms = jnp.mean(x * x, axis=-1, keepdims=True)
    return x * jax.lax.rsqrt(ms + NORM_EPS) * g


def _dot(a, b):
    return jnp.dot(a, b, preferred_element_type=jnp.float32)


def _dot_nt(a, b):
    return jax.lax.dot_general(a, b, (((1,), (1,)), ((), ())),
                               preferred_element_type=jnp.float32)


def _params(n_axes):
    return pltpu.CompilerParams(
        dimension_semantics=("arbitrary",) * n_axes,
        vmem_limit_bytes=VMEM_LIMIT_BYTES)


def _resident(shape):
    return pl.BlockSpec(shape, lambda *_: (0,) * len(shape), pipeline_mode=pl.Buffered(1))


def _slab(j):
    return slice(j * LANES, (j + 1) * LANES)


def _split_streams(nat_ref, out4_ref, out16_ref, by4_ref):
    tm = nat_ref.shape[1]
    n4, n16 = tm // 4, tm // 16
    for r in range(RELAYOUT_RADIX):
        for j in range(N_SLABS):
            piece = nat_ref[j, pl.ds(r, n4, stride=RELAYOUT_RADIX), :]
            by4_ref[j, r * n4:(r + 1) * n4, :] = piece
            out4_ref[r, :, _slab(j)] = piece.astype(jnp.bfloat16)
    for r4 in range(RELAYOUT_RADIX):
        for a in range(RELAYOUT_RADIX):
            for j in range(N_SLABS):
                piece = by4_ref[j, pl.ds(r4 * n4 + a, n16, stride=RELAYOUT_RADIX), :]
                out16_ref[RELAYOUT_RADIX * a + r4, :, _slab(j)] = piece.astype(jnp.bfloat16)


def _merge_streams4(in4_ref, nat_ref):
    n4 = in4_ref.shape[1]
    for r in range(RELAYOUT_RADIX):
        for j in range(nat_ref.shape[0]):
            nat_ref[j, pl.ds(r, n4, stride=RELAYOUT_RADIX), :] = in4_ref[r, :, _slab(j)]


def _merge_streams16(in16_ref, by4_ref, nat_ref):
    n16 = in16_ref.shape[1]
    n4 = RELAYOUT_RADIX * n16
    n_slabs = nat_ref.shape[0]
    for r4 in range(RELAYOUT_RADIX):
        for a in range(RELAYOUT_RADIX):
            for j in range(n_slabs):
                by4_ref[j, pl.ds(r4 * n4 + a, n16, stride=RELAYOUT_RADIX), :] = (
                    in16_ref[RELAYOUT_RADIX * a + r4, :, _slab(j)])
    for r4 in range(RELAYOUT_RADIX):
        for j in range(n_slabs):
            nat_ref[j, pl.ds(r4, n4, stride=RELAYOUT_RADIX), :] = (
                by4_ref[j, r4 * n4:(r4 + 1) * n4, :])


def _in_proj_kernel(tiles_per_seq, x_ref, g_ref, w_ref, cw_ref, gc_ref,
                    q1_ref, k1_ref, v1_ref, q4_ref, k4_ref, v4_ref,
                    q16_ref, k16_ref, v16_ref, c_ref, ubuf_ref, nat_ref, by4_ref):
    tm = x_ref.shape[0]

    @pl.when(pl.program_id(0) == 0)
    def _():
        ubuf_ref[tm:tm + F32_SUBLANES, :] = jnp.zeros((F32_SUBLANES, CONV_WIDTH), jnp.float32)

    h = _rms(x_ref[...], g_ref[...]).astype(jnp.bfloat16)

    def proj(col):
        return _dot(h, w_ref[:, col * ATT_WIDTH:(col + 1) * ATT_WIDTH])

    def project_streams(col, out1_ref, out4_ref, out16_ref, scale=None):
        val = proj(col) if scale is None else proj(col) * scale
        out1_ref[...] = val.astype(jnp.bfloat16)
        for j in range(N_SLABS):
            nat_ref[j] = val[:, _slab(j)]
        _split_streams(nat_ref, out4_ref, out16_ref, by4_ref)

    project_streams(0, q1_ref, q4_ref, q16_ref, LOG2_E / math.sqrt(HEAD_DIM))
    project_streams(1, k1_ref, k4_ref, k16_ref)
    project_streams(2, v1_ref, v4_ref, v16_ref)

    seq_start = pl.program_id(0) % tiles_per_seq == 0
    ubuf_ref[0:F32_SUBLANES, :] = jnp.where(seq_start, 0.0, ubuf_ref[tm:tm + F32_SUBLANES, :])
    u = proj(4) * proj(5)
    ubuf_ref[F32_SUBLANES:tm + F32_SUBLANES, :] = u
    conv = (u * cw_ref[2:3, :]
            + ubuf_ref[F32_SUBLANES - 1:tm + F32_SUBLANES - 1, :] * cw_ref[1:2, :]
            + ubuf_ref[F32_SUBLANES - 2:tm + F32_SUBLANES - 2, :] * cw_ref[0:1, :])
    y = proj(3) * conv
    c_ref[...] = _rms(y, gc_ref[...]).astype(jnp.bfloat16)


def _stream_spec(dilation, tm, tiles_per_seq):
    return pl.BlockSpec((None, dilation, tm // dilation, ATT_WIDTH),
                        lambda i: (i // tiles_per_seq, 0, i % tiles_per_seq, 0))


def _in_proj(x2d, g_mix, w_in, conv_w, g_conv_out, batch, seq_len):
    t = x2d.shape[0]
    tm = TOKEN_TILE
    assert seq_len % tm == 0 and t == batch * seq_len
    tiles_per_seq = seq_len // tm
    nat = jax.ShapeDtypeStruct((t, ATT_WIDTH), jnp.bfloat16)
    by = lambda d: jax.ShapeDtypeStruct((batch, d, seq_len // d, ATT_WIDTH), jnp.bfloat16)
    row_spec = pl.BlockSpec((tm, ATT_WIDTH), lambda i: (i, 0))
    s4, s16 = _stream_spec(4, tm, tiles_per_seq), _stream_spec(16, tm, tiles_per_seq)
    return pl.pallas_call(
        functools.partial(_in_proj_kernel, tiles_per_seq),
        grid=(t // tm,),
        in_specs=[
            pl.BlockSpec((tm, D_MODEL), lambda i: (i, 0)),
            _resident((1, D_MODEL)),
            _resident(w_in.shape),
            _resident(conv_w.shape),
            _resident((1, CONV_WIDTH)),
        ],
        out_specs=[row_spec] * 3 + [s4] * 3 + [s16] * 3 + [row_spec],
        out_shape=[nat] * 3 + [by(4)] * 3 + [by(16)] * 3 + [nat],
        scratch_shapes=[pltpu.VMEM((tm + F32_SUBLANES, CONV_WIDTH), jnp.float32),
                        pltpu.VMEM((N_SLABS, tm, LANES), jnp.float32),
                        pltpu.VMEM((N_SLABS, tm, LANES), jnp.float32)],
        compiler_params=_params(1),
        name="in_proj",
    )(x2d, g_mix.reshape(1, -1), w_in, conv_w, g_conv_out.reshape(1, -1))


def _attn_kernel(q_ref, kc_ref, kp_ref, vc_ref, vp_ref, o_ref, lse_ref):
    s = BAND_STEPS
    streams, row_blocks = q_ref.shape[0], q_ref.shape[1] // s
    has_prev = pl.program_id(2) > 0
    qi = jax.lax.broadcasted_iota(jnp.int32, (2 * s, 2 * s), 0) & (s - 1)
    kj = jax.lax.broadcasted_iota(jnp.int32, (2 * s, 2 * s), 1)
    band = (kj >= qi) & (kj <= qi + s)
    cap = jnp.where(band, jnp.inf, NEG_INF)
    cap_first = jnp.where(band & ((kj >= s) | has_prev), jnp.inf, NEG_INF)
    first_head = jax.lax.broadcasted_iota(jnp.int32, (s, PAIR_LANES), 1) < HEAD_DIM
    ones = jnp.ones((2 * s, PAIR_LANES), jnp.bfloat16)

    def rows_of(i):
        return slice(i * s, (i + 1) * s)

    def scores(r, i, pair):
        lanes = _slab(pair)
        q2 = q_ref[r, rows_of(i), lanes]
        zero = jnp.zeros_like(q2)
        q = jnp.concatenate([jnp.where(first_head, q2, zero),
                             jnp.where(first_head, zero, q2)], axis=0)
        k_prev = kp_ref[r, :, lanes] if i == 0 else kc_ref[r, rows_of(i - 1), lanes]
        k = jnp.concatenate([k_prev, kc_ref[r, rows_of(i), lanes]], axis=0)
        return jnp.minimum(_dot_nt(q, k), cap_first if i == 0 else cap)

    def finish(r, i, pair, sc):
        lanes = _slab(pair)
        v_prev = vp_ref[r, :, lanes] if i == 0 else vc_ref[r, rows_of(i - 1), lanes]
        v = jnp.concatenate(
            [jnp.concatenate([v_prev, vc_ref[r, rows_of(i), lanes]], axis=0), ones], axis=1)
        m = jnp.max(sc, axis=-1, keepdims=True)
        p = jnp.exp2(sc - m)
        pv = _dot(p.astype(jnp.bfloat16), v)
        l = pv[:, PAIR_LANES:]
        o = pv[:, :PAIR_LANES] / l
        o_ref[r, rows_of(i), lanes] = jnp.where(first_head, o[:s], o[s:])
        lse = m + jnp.log2(l)
        for half in range(2):
            at = slice((2 * pair + half) * LSE_REP, (2 * pair + half + 1) * LSE_REP)
            lse_ref[r, rows_of(i), at] = lse[half * s:(half + 1) * s, at]

    items = [(r, i, pair) for r in range(streams) for i in range(row_blocks)
             for pair in range(N_ATT_HEADS // 2)]
    sc = scores(*items[0])
    for n, item in enumerate(items):
        sc_next = scores(*items[n + 1]) if n + 1 < len(items) else None
        finish(*item, sc)
        sc = sc_next


def _dilated_attention(q, k, v):
    b, d, steps, _ = q.shape
    row_blocks = min(ATTN_SUBTILES, steps // BAND_STEPS)
    streams = ATTN_SUBTILES // row_blocks
    tile_rows = row_blocks * BAND_STEPS
    assert steps % tile_rows == 0 and d % streams == 0
    cur = pl.BlockSpec((None, streams, tile_rows, ATT_WIDTH), lambda bi, g, n: (bi, g, n, 0))
    prev = pl.BlockSpec((None, streams, BAND_STEPS, ATT_WIDTH),
                        lambda bi, g, n: (bi, g, jnp.maximum(n * row_blocks - 1, 0), 0))
    cur_lse = pl.BlockSpec((None, streams, tile_rows, LANES), lambda bi, g, n: (bi, g, n, 0))
    return pl.pallas_call(
        _attn_kernel,
        grid=(b, d // streams, steps // tile_rows),
        in_specs=[cur, cur, prev, cur, prev],
        out_specs=[cur, cur_lse],
        out_shape=[jax.ShapeDtypeStruct(q.shape, jnp.float32),
                   jax.ShapeDtypeStruct((b, d, steps, LANES), jnp.float32)],
        compiler_params=_params(3),
        name=f"dilated_attn_d{d}",
    )(q, k, k, v, v)


def _post_attn_kernel(o1_ref, l1_ref, o4_ref, l4_ref, o16_ref, l16_ref, c_ref, x_ref,
                      ga_ref, wout_ref, gx_ref, wq_ref, kv_ref, wo_ref,
                      gm_ref, wu_ref, wd_ref, gf_ref, y_ref,
                      on4_ref, ln4_ref, on16_ref, ln16_ref, by4_ref, xmid_ref):
    i = pl.program_id(0)

    @pl.when(i == 0)
    def _():
        xmid_ref[...] = jnp.zeros(xmid_ref.shape, jnp.float32)

    assert sum(MLP_CHUNKS_PER_STAGE) * FF_CHUNK == D_FF
    y = xmid_ref[(i + 1) % 2]
    h_mlp = _rms(y, gm_ref[...]).astype(jnp.bfloat16)
    chunk_ids = iter(range(D_FF // FF_CHUNK))

    def mlp_stage(y, stage):
        for _ in range(MLP_CHUNKS_PER_STAGE[stage]):
            c = next(chunk_ids)
            cols = slice(c * FF_CHUNK, (c + 1) * FF_CHUNK)
            a = jnp.maximum(_dot(h_mlp, wu_ref[:, cols]), 0.0)
            y = y + _dot((a * a).astype(jnp.bfloat16), wd_ref[cols, :])
        return y

    _merge_streams4(o4_ref, on4_ref)
    _merge_streams4(l4_ref, ln4_ref)
    _merge_streams16(o16_ref, by4_ref, on16_ref)
    _merge_streams16(l16_ref, by4_ref, ln16_ref)
    lses = [l1_ref[...], ln4_ref[0], ln16_ref[0]]
    top = jnp.maximum(jnp.maximum(lses[0], lses[1]), lses[2])
    es = [jnp.exp2(l - top) for l in lses]
    inv = 1.0 / (es[0] + es[1] + es[2])
    weights = [e * inv for e in es]
    lane = jax.lax.broadcasted_iota(jnp.int32, weights[0].shape, 1)
    slabs = []
    for j in range(N_SLABS):
        src = (2 * j + lane // HEAD_DIM) * LSE_REP
        w = [jnp.take_along_axis(wp, src, axis=1) for wp in weights]
        slabs.append(w[0] * o1_ref[:, _slab(j)] + w[1] * on4_ref[j] + w[2] * on16_ref[j])
    attn = jnp.concatenate(slabs, axis=-1)
    attn_n = _rms(attn, ga_ref[...]).astype(jnp.bfloat16)
    y = mlp_stage(y, 0)
    x = (x_ref[...] + _dot(attn_n, wout_ref[0:ATT_WIDTH, :])
         + _dot(c_ref[...], wout_ref[ATT_WIDTH:, :]))

    y = mlp_stage(y, 1)

    h = _rms(x, gx_ref[...]).astype(jnp.bfloat16)
    q = (_dot(h, wq_ref[...]) * (LOG2_E / math.sqrt(MEM_HEAD_DIM))).astype(jnp.bfloat16)
    head = lambda hd: slice(hd * MEM_HEAD_DIM, (hd + 1) * MEM_HEAD_DIM)
    scores = [_dot_nt(q[:, head(hd)], kv_ref[:, head(hd)]) for hd in range(N_MEM_HEADS)]

    y = mlp_stage(y, 2)
    y_ref[...] = _rms(y, gf_ref[...])

    heads = []
    for hd in range(N_MEM_HEADS):
        sc = scores[hd]
        m = jnp.max(sc, axis=-1, keepdims=True)
        p = jnp.exp2(sc - m)
        l = jnp.sum(p, axis=-1, keepdims=True)
        v = kv_ref[:, D_MODEL + hd * MEM_HEAD_DIM:D_MODEL + (hd + 1) * MEM_HEAD_DIM]
        heads.append((_dot(p.astype(jnp.bfloat16), v) / l).astype(jnp.bfloat16))
    xmid_ref[i % 2] = x + _dot(jnp.concatenate(heads, axis=-1), wo_ref[...])


def _post_attn(o1, l1, o4, l4, o16, l16, convn, x2d, kv, g_attn_out, w_out, g_xattn, w_q, w_o,
               g_mlp, w_up, w_down, g_final, seq_len, mem_len):
    t = x2d.shape[0]
    tm = POST_TILE
    assert seq_len % tm == 0
    tiles_per_seq = seq_len // tm
    n_tiles = t // tm
    tile_in = lambda i: jnp.minimum(i, n_tiles - 1)
    row_spec = lambda width: pl.BlockSpec((tm, width), lambda i: (tile_in(i), 0))
    stream_spec = lambda d, width: pl.BlockSpec(
        (None, d, tm // d, width),
        lambda i: (tile_in(i) // tiles_per_seq, 0, tile_in(i) % tiles_per_seq, 0))
    x_spec = pl.BlockSpec((tm, D_MODEL), lambda i: (tile_in(i), 0))
    gain = lambda g: g.reshape(1, -1)
    nat = pltpu.VMEM((N_SLABS, tm, LANES), jnp.float32)
    nat_lse = pltpu.VMEM((1, tm, LANES), jnp.float32)
    return pl.pallas_call(
        _post_attn_kernel,
        grid=(n_tiles + 1,),
        in_specs=[row_spec(ATT_WIDTH), row_spec(LANES),
                  stream_spec(4, ATT_WIDTH), stream_spec(4, LANES),
                  stream_spec(16, ATT_WIDTH), stream_spec(16, LANES),
                  row_spec(ATT_WIDTH), x_spec,
                  _resident((1, ATT_WIDTH)), _resident(w_out.shape),
                  _resident((1, D_MODEL)), _resident(w_q.shape),
                  pl.BlockSpec((mem_len, 2 * D_MODEL),
                               lambda i: (tile_in(i) // tiles_per_seq, 0)),
                  _resident(w_o.shape),
                  _resident((1, D_MODEL)), _resident(w_up.shape), _resident(w_down.shape),
                  _resident((1, D_MODEL))],
        out_specs=pl.BlockSpec((tm, D_MODEL), lambda i: (jnp.maximum(i - 1, 0), 0)),
        out_shape=jax.ShapeDtypeStruct((t, D_MODEL), jnp.float32),
        scratch_shapes=[nat, nat_lse, nat, nat_lse, nat,
                        pltpu.VMEM((2, tm, D_MODEL), jnp.float32)],
        compiler_params=_params(1),
        name="post_attn",
    )(o1, l1, o4, l4, o16, l16, convn, x2d, gain(g_attn_out), w_out, gain(g_xattn), w_q, kv, w_o,
      gain(g_mlp), w_up, w_down, gain(g_final))


def _mem_kv_kernel(m_ref, g_ref, w_ref, kv_ref):
    h = _rms(m_ref[...], g_ref[...]).astype(jnp.bfloat16)
    kv_ref[...] = _dot(h, w_ref[...]).astype(jnp.bfloat16)


def _mem_kv(mem2d, g_mem, w_kv):
    rows = mem2d.shape[0]
    tm = TOKEN_TILE
    return pl.pallas_call(
        _mem_kv_kernel,
        grid=(rows // tm,),
        in_specs=[pl.BlockSpec((tm, D_MODEL), lambda i: (i, 0)),
                  _resident((1, D_MODEL)), _resident(w_kv.shape)],
        out_specs=pl.BlockSpec((tm, 2 * D_MODEL), lambda i: (i, 0)),
        out_shape=jax.ShapeDtypeStruct((rows, 2 * D_MODEL), jnp.bfloat16),
        compiler_params=_params(1),
        name="mem_kv_proj",
    )(mem2d, g_mem.reshape(1, -1), w_kv)


def kernel(x, mem, g_mix, w_in, conv_w, g_attn_out, g_conv_out, w_out, g_xattn, g_mem,
           w_q_mem, w_kv_mem, w_o_mem, g_mlp, w_up, w_down, g_final):
    b, seq, d = x.shape
    mem_len = mem.shape[1]
    assert tuple(dil for _, dil in DILATED_PATTERNS) == (1, 4, 16)
    assert all(win // dil == BAND_STEPS for win, dil in DILATED_PATTERNS)
    bf = lambda w: w.astype(jnp.bfloat16)
    x2d = x.reshape(b * seq, d)

    (q1, k1, v1, q4, k4, v4, q16, k16, v16, convn) = _in_proj(
        x2d, g_mix, bf(w_in), conv_w, g_conv_out, b, seq)
    as_stream = lambda a: a.reshape(b, 1, seq, ATT_WIDTH)
    o1, l1 = _dilated_attention(as_stream(q1), as_stream(k1), as_stream(v1))
    o4, l4 = _dilated_attention(q4, k4, v4)
    o16, l16 = _dilated_attention(q16, k16, v16)
    flat = lambda a: a.reshape(b * seq, a.shape[-1])
    kv = _mem_kv(mem.reshape(b * mem_len, d), g_mem, bf(w_kv_mem))
    y = _post_attn(flat(o1), flat(l1), o4, l4, o16, l16, convn, x2d, kv,
                   g_attn_out, bf(w_out), g_xattn, bf(w_q_mem), bf(w_o_mem),
                   g_mlp, bf(w_up), bf(w_down), g_final, seq, mem_len)
    return y.reshape(b, seq, d)
```

```python
import functools
import math

import jax
import jax.numpy as jnp
from jax.experimental import pallas as pl
from jax.experimental.pallas import tpu as pltpu

D_MODEL = 1024
ATT_WIDTH = 512
HEAD_DIM = 64
N_ATT_HEADS = 8
CONV_WIDTH = 512
CONV_K = 3
DILATED_PATTERNS = ((128, 1), (512, 4), (2048, 16))
BAND_STEPS = 128
N_MEM_HEADS = 4
MEM_HEAD_DIM = 256
D_FF = 4096
NORM_EPS = 1e-6
NEG_INF = -1e30
LOG2_E = math.log2(math.e)

LANES = 128
F32_SUBLANES = 8
PAIR_LANES = 2 * HEAD_DIM
assert PAIR_LANES == LANES
N_SLABS = ATT_WIDTH // LANES
LSE_REP = LANES // N_ATT_HEADS
VMEM_LIMIT_BYTES = 58 * 1024 * 1024

TOKEN_TILE = 1024
POST_TILE = 512
FF_CHUNK = 1024
MLP_CHUNKS_PER_STAGE = (2, 1, 1)
ATTN_SUBTILES = 16
RELAYOUT_RADIX = 4


def _rms(x, g):
    ms = jnp.mean(x * x, axis=-1, keepdims=True)
    return x * jax.lax.rsqrt(ms + NORM_EPS) * g


def _dot(a, b):
    return jnp.dot(a, b, preferred_element_type=jnp.float32)


def _dot_nt(a, b):
    return jax.lax.dot_general(a, b, (((1,), (1,)), ((), ())),
                               preferred_element_type=jnp.float32)


def _params(n_axes):
    return pltpu.CompilerParams(
        dimension_semantics=("arbitrary",) * n_axes,
        vmem_limit_bytes=VMEM_LIMIT_BYTES)


def _resident(shape):
    return pl.BlockSpec(shape, lambda *_: (0,) * len(shape), pipeline_mode=pl.Buffered(1))


def _slab(j):
    return slice(j * LANES, (j + 1) * LANES)


def _split_streams(nat_ref, out4_ref, out16_ref, by4_ref):
    tm = nat_ref.shape[1]
    n4, n16 = tm // 4, tm // 16
    for r in range(RELAYOUT_RADIX):
        for j in range(N_SLABS):
            piece = nat_ref[j, pl.ds(r, n4, stride=RELAYOUT_RADIX), :]
            by4_ref[j, r * n4:(r + 1) * n4, :] = piece
            out4_ref[r, :, _slab(j)] = piece.astype(jnp.bfloat16)
    for r4 in range(RELAYOUT_RADIX):
        for a in range(RELAYOUT_RADIX):
            for j in range(N_SLABS):
                piece = by4_ref[j, pl.ds(r4 * n4 + a, n16, stride=RELAYOUT_RADIX), :]
                out16_ref[RELAYOUT_RADIX * a + r4, :, _slab(j)] = piece.astype(jnp.bfloat16)


def _merge_streams4(in4_ref, nat_ref):
    n4 = in4_ref.shape[1]
    for r in range(RELAYOUT_RADIX):
        for j in range(nat_ref.shape[0]):
            nat_ref[j, pl.ds(r, n4, stride=RELAYOUT_RADIX), :] = in4_ref[r, :, _slab(j)]


def _merge_streams16(in16_ref, by4_ref, nat_ref):
    n16 = in16_ref.shape[1]
    n4 = RELAYOUT_RADIX * n16
    n_slabs = nat_ref.shape[0]
    for r4 in range(RELAYOUT_RADIX):
        for a in range(RELAYOUT_RADIX):
            for j in range(n_slabs):
                by4_ref[j, pl.ds(r4 * n4 + a, n16, stride=RELAYOUT_RADIX), :] = (
                    in16_ref[RELAYOUT_RADIX * a + r4, :, _slab(j)])
    for r4 in range(RELAYOUT_RADIX):
        for j in range(n_slabs):
            nat_ref[j, pl.ds(r4, n4, stride=RELAYOUT_RADIX), :] = (
                by4_ref[j, r4 * n4:(r4 + 1) * n4, :])


def _in_proj_kernel(tiles_per_seq, x_ref, g_ref, w_ref, cw_ref, gc_ref,
                    q1_ref, k1_ref, v1_ref, q4_ref, k4_ref, v4_ref,
                    q16_ref, k16_ref, v16_ref, c_ref, ubuf_ref, nat_ref, by4_ref):
    tm = x_ref.shape[0]

    @pl.when(pl.program_id(0) == 0)
    def _():
        ubuf_ref[tm:tm + F32_SUBLANES, :] = jnp.zeros((F32_SUBLANES, CONV_WIDTH), jnp.float32)

    h = _rms(x_ref[...], g_ref[...]).astype(jnp.bfloat16)

    def proj(col):
        return _dot(h, w_ref[:, col * ATT_WIDTH:(col + 1) * ATT_WIDTH])

    def project_streams(col, out1_ref, out4_ref, out16_ref, scale=None):
        val = proj(col) if scale is None else proj(col) * scale
        out1_ref[...] = val.astype(jnp.bfloat16)
        for j in range(N_SLABS):
            nat_ref[j] = val[:, _slab(j)]
        _split_streams(nat_ref, out4_ref, out16_ref, by4_ref)

    project_streams(0, q1_ref, q4_ref, q16_ref, LOG2_E / math.sqrt(HEAD_DIM))
    project_streams(1, k1_ref, k4_ref, k16_ref)
    project_streams(2, v1_ref, v4_ref, v16_ref)

    seq_start = pl.program_id(0) % tiles_per_seq == 0
    ubuf_ref[0:F32_SUBLANES, :] = jnp.where(seq_start, 0.0, ubuf_ref[tm:tm + F32_SUBLANES, :])
    u = proj(4) * proj(5)
    ubuf_ref[F32_SUBLANES:tm + F32_SUBLANES, :] = u
    conv = (u * cw_ref[2:3, :]
            + ubuf_ref[F32_SUBLANES - 1:tm + F32_SUBLANES - 1, :] * cw_ref[1:2, :]
            + ubuf_ref[F32_SUBLANES - 2:tm + F32_SUBLANES - 2, :] * cw_ref[0:1, :])
    y = proj(3) * conv
    c_ref[...] = _rms(y, gc_ref[...]).astype(jnp.bfloat16)


def _stream_spec(dilation, tm, tiles_per_seq):
    return pl.BlockSpec((None, dilation, tm // dilation, ATT_WIDTH),
                        lambda i: (i // tiles_per_seq, 0, i % tiles_per_seq, 0))


def _in_proj(x2d, g_mix, w_in, conv_w, g_conv_out, batch, seq_len):
    t = x2d.shape[0]
    tm = TOKEN_TILE
    assert seq_len % tm == 0 and t == batch * seq_len
    tiles_per_seq = seq_len // tm
    nat = jax.ShapeDtypeStruct((t, ATT_WIDTH), jnp.bfloat16)
    by = lambda d: jax.ShapeDtypeStruct((batch, d, seq_len // d, ATT_WIDTH), jnp.bfloat16)
    row_spec = pl.BlockSpec((tm, ATT_WIDTH), lambda i: (i, 0))
    s4, s16 = _stream_spec(4, tm, tiles_per_seq), _stream_spec(16, tm, tiles_per_seq)
    return pl.pallas_call(
        functools.partial(_in_proj_kernel, tiles_per_seq),
        grid=(t // tm,),
        in_specs=[
            pl.BlockSpec((tm, D_MODEL), lambda i: (i, 0)),
            _resident((1, D_MODEL)),
            _resident(w_in.shape),
            _resident(conv_w.shape),
            _resident((1, CONV_WIDTH)),
        ],
        out_specs=[row_spec] * 3 + [s4] * 3 + [s16] * 3 + [row_spec],
        out_shape=[nat] * 3 + [by(4)] * 3 + [by(16)] * 3 + [nat],
        scratch_shapes=[pltpu.VMEM((tm + F32_SUBLANES, CONV_WIDTH), jnp.float32),
                        pltpu.VMEM((N_SLABS, tm, LANES), jnp.float32),
                        pltpu.VMEM((N_SLABS, tm, LANES), jnp.float32)],
        compiler_params=_params(1),
        name="in_proj",
    )(x2d, g_mix.reshape(1, -1), w_in, conv_w, g_conv_out.reshape(1, -1))


def _attn_kernel(q_ref, kc_ref, kp_ref, vc_ref, vp_ref, o_ref, lse_ref):
    s = BAND_STEPS
    streams, row_blocks = q_ref.shape[0], q_ref.shape[1] // s
    has_prev = pl.program_id(2) > 0
    qi = jax.lax.broadcasted_iota(jnp.int32, (2 * s, 2 * s), 0) & (s - 1)
    kj = jax.lax.broadcasted_iota(jnp.int32, (2 * s, 2 * s), 1)
    band = (kj >= qi) & (kj <= qi + s)
    cap = jnp.where(band, jnp.inf, NEG_INF)
    cap_first = jnp.where(band & ((kj >= s) | has_prev), jnp.inf, NEG_INF)
    first_head = jax.lax.broadcasted_iota(jnp.int32, (s, PAIR_LANES), 1) < HEAD_DIM
    ones = jnp.ones((2 * s, PAIR_LANES), jnp.bfloat16)

    def rows_of(i):
        return slice(i * s, (i + 1) * s)

    def scores(r, i, pair):
        lanes = _slab(pair)
        q2 = q_ref[r, rows_of(i), lanes]
        zero = jnp.zeros_like(q2)
        q = jnp.concatenate([jnp.where(first_head, q2, zero),
                             jnp.where(first_head, zero, q2)], axis=0)
        k_prev = kp_ref[r, :, lanes] if i == 0 else kc_ref[r, rows_of(i - 1), lanes]
        k = jnp.concatenate([k_prev, kc_ref[r, rows_of(i), lanes]], axis=0)
        return jnp.minimum(_dot_nt(q, k), cap_first if i == 0 else cap)

    def finish(r, i, pair, sc):
        lanes = _slab(pair)
        v_prev = vp_ref[r, :, lanes] if i == 0 else vc_ref[r, rows_of(i - 1), lanes]
        v = jnp.concatenate(
            [jnp.concatenate([v_prev, vc_ref[r, rows_of(i), lanes]], axis=0), ones], axis=1)
        m = jnp.max(sc, axis=-1, keepdims=True)
        p = jnp.exp2((sc - m).astype(jnp.bfloat16))
        pv = _dot(p, v)
        l = pv[:, PAIR_LANES:]
        o = pv[:, :PAIR_LANES] / l
        o_ref[r, rows_of(i), lanes] = jnp.where(first_head, o[:s], o[s:])
        lse = m + jnp.log2(l)
        for half in range(2):
            at = slice((2 * pair + half) * LSE_REP, (2 * pair + half + 1) * LSE_REP)
            lse_ref[r, rows_of(i), at] = lse[half * s:(half + 1) * s, at]

    items = [(r, i, pair) for r in range(streams) for i in range(row_blocks)
             for pair in range(N_ATT_HEADS // 2)]
    sc = scores(*items[0])
    for n, item in enumerate(items):
        sc_next = scores(*items[n + 1]) if n + 1 < len(items) else None
        finish(*item, sc)
        sc = sc_next


def _dilated_attention(q, k, v):
    b, d, steps, _ = q.shape
    row_blocks = min(ATTN_SUBTILES, steps // BAND_STEPS)
    streams = ATTN_SUBTILES // row_blocks
    tile_rows = row_blocks * BAND_STEPS
    assert steps % tile_rows == 0 and d % streams == 0
    cur = pl.BlockSpec((None, streams, tile_rows, ATT_WIDTH), lambda bi, g, n: (bi, g, n, 0))
    prev = pl.BlockSpec((None, streams, BAND_STEPS, ATT_WIDTH),
                        lambda bi, g, n: (bi, g, jnp.maximum(n * row_blocks - 1, 0), 0))
    cur_lse = pl.BlockSpec((None, streams, tile_rows, LANES), lambda bi, g, n: (bi, g, n, 0))
    return pl.pallas_call(
        _attn_kernel,
        grid=(b, d // streams, steps // tile_rows),
        in_specs=[cur, cur, prev, cur, prev],
        out_specs=[cur, cur_lse],
        out_shape=[jax.ShapeDtypeStruct(q.shape, jnp.float32),
                   jax.ShapeDtypeStruct((b, d, steps, LANES), jnp.float32)],
        compiler_params=_params(3),
        name=f"dilated_attn_d{d}",
    )(q, k, k, v, v)


def _post_attn_kernel(o1_ref, l1_ref, o4_ref, l4_ref, o16_ref, l16_ref, c_ref, x_ref,
                      ga_ref, wout_ref, gx_ref, wq_ref, kv_ref, wo_ref,
                      gm_ref, wu_ref, wd_ref, gf_ref, y_ref,
                      on4_ref, ln4_ref, on16_ref, ln16_ref, by4_ref, xmid_ref):
    i = pl.program_id(0)

    @pl.when(i == 0)
    def _():
        xmid_ref[...] = jnp.zeros(xmid_ref.shape, jnp.float32)

    assert sum(MLP_CHUNKS_PER_STAGE) * FF_CHUNK == D_FF
    y = xmid_ref[(i + 1) % 2]
    h_mlp = _rms(y, gm_ref[...]).astype(jnp.bfloat16)
    chunk_ids = iter(range(D_FF // FF_CHUNK))

    def mlp_stage(y, stage):
        for _ in range(MLP_CHUNKS_PER_STAGE[stage]):
            c = next(chunk_ids)
            cols = slice(c * FF_CHUNK, (c + 1) * FF_CHUNK)
            a = jnp.maximum(_dot(h_mlp, wu_ref[:, cols]), 0.0)
            y = y + _dot((a * a).astype(jnp.bfloat16), wd_ref[cols, :])
        return y

    _merge_streams4(o4_ref, on4_ref)
    _merge_streams4(l4_ref, ln4_ref)
    _merge_streams16(o16_ref, by4_ref, on16_ref)
    _merge_streams16(l16_ref, by4_ref, ln16_ref)
    lses = [l1_ref[...], ln4_ref[0], ln16_ref[0]]
    top = jnp.maximum(jnp.maximum(lses[0], lses[1]), lses[2])
    es = [jnp.exp2(l - top) for l in lses]
    inv = 1.0 / (es[0] + es[1] + es[2])
    weights = [e * inv for e in es]
    lane = jax.lax.broadcasted_iota(jnp.int32, weights[0].shape, 1)
    slabs = []
    for j in range(N_SLABS):
        src = (2 * j + lane // HEAD_DIM) * LSE_REP
        w = [jnp.take_along_axis(wp, src, axis=1) for wp in weights]
        slabs.append(w[0] * o1_ref[:, _slab(j)] + w[1] * on4_ref[j] + w[2] * on16_ref[j])
    attn = jnp.concatenate(slabs, axis=-1)
    attn_n = _rms(attn, ga_ref[...]).astype(jnp.bfloat16)
    y = mlp_stage(y, 0)
    x = (x_ref[...] + _dot(attn_n, wout_ref[0:ATT_WIDTH, :])
         + _dot(c_ref[...], wout_ref[ATT_WIDTH:, :]))

    y = mlp_stage(y, 1)

    h = _rms(x, gx_ref[...]).astype(jnp.bfloat16)
    q = (_dot(h, wq_ref[...]) * (LOG2_E / math.sqrt(MEM_HEAD_DIM))).astype(jnp.bfloat16)
    head = lambda hd: slice(hd * MEM_HEAD_DIM, (hd + 1) * MEM_HEAD_DIM)
    scores = [_dot_nt(q[:, head(hd)], kv_ref[:, head(hd)]) for hd in range(N_MEM_HEADS)]

    y = mlp_stage(y, 2)
    y_ref[...] = _rms(y, gf_ref[...])

    heads = []
    for hd in range(N_MEM_HEADS):
        sc = scores[hd]
        m = jnp.max(sc, axis=-1, keepdims=True)
        p = jnp.exp2(sc - m)
        l = jnp.sum(p, axis=-1, keepdims=True)
        v = kv_ref[:, D_MODEL + hd * MEM_HEAD_DIM:D_MODEL + (hd + 1) * MEM_HEAD_DIM]
        heads.append((_dot(p.astype(jnp.bfloat16), v) / l).astype(jnp.bfloat16))
    xmid_ref[i % 2] = x + _dot(jnp.concatenate(heads, axis=-1), wo_ref[...])


def _post_attn(o1, l1, o4, l4, o16, l16, convn, x2d, kv, g_attn_out, w_out, g_xattn, w_q, w_o,
               g_mlp, w_up, w_down, g_final, seq_len, mem_len):
    t = x2d.shape[0]
    tm = POST_TILE
    assert seq_len % tm == 0
    tiles_per_seq = seq_len // tm
    n_tiles = t // tm
    tile_in = lambda i: jnp.minimum(i, n_tiles - 1)
    row_spec = lambda width: pl.BlockSpec((tm, width), lambda i: (tile_in(i), 0))
    stream_spec = lambda d, width: pl.BlockSpec(
        (None, d, tm // d, width),
        lambda i: (tile_in(i) // tiles_per_seq, 0, tile_in(i) % tiles_per_seq, 0))
    x_spec = pl.BlockSpec((tm, D_MODEL), lambda i: (tile_in(i), 0))
    gain = lambda g: g.reshape(1, -1)
    nat = pltpu.VMEM((N_SLABS, tm, LANES), jnp.float32)
    nat_lse = pltpu.VMEM((1, tm, LANES), jnp.float32)
    return pl.pallas_call(
        _post_attn_kernel,
        grid=(n_tiles + 1,),
        in_specs=[row_spec(ATT_WIDTH), row_spec(LANES),
                  stream_spec(4, ATT_WIDTH), stream_spec(4, LANES),
                  stream_spec(16, ATT_WIDTH), stream_spec(16, LANES),
                  row_spec(ATT_WIDTH), x_spec,
                  _resident((1, ATT_WIDTH)), _resident(w_out.shape),
                  _resident((1, D_MODEL)), _resident(w_q.shape),
                  pl.BlockSpec((mem_len, 2 * D_MODEL),
                               lambda i: (tile_in(i) // tiles_per_seq, 0)),
                  _resident(w_o.shape),
                  _resident((1, D_MODEL)), _resident(w_up.shape), _resident(w_down.shape),
                  _resident((1, D_MODEL))],
        out_specs=pl.BlockSpec((tm, D_MODEL), lambda i: (jnp.maximum(i - 1, 0), 0)),
        out_shape=jax.ShapeDtypeStruct((t, D_MODEL), jnp.float32),
        scratch_shapes=[nat, nat_lse, nat, nat_lse, nat,
                        pltpu.VMEM((2, tm, D_MODEL), jnp.float32)],
        compiler_params=_params(1),
        name="post_attn",
    )(o1, l1, o4, l4, o16, l16, convn, x2d, gain(g_attn_out), w_out, gain(g_xattn), w_q, kv, w_o,
      gain(g_mlp), w_up, w_down, gain(g_final))


def _mem_kv_kernel(m_ref, g_ref, w_ref, kv_ref):
    h = _rms(m_ref[...], g_ref[...]).astype(jnp.bfloat16)
    kv_ref[...] = _dot(h, w_ref[...]).astype(jnp.bfloat16)


def _mem_kv(mem2d, g_mem, w_kv):
    rows = mem2d.shape[0]
    tm = TOKEN_TILE
    return pl.pallas_call(
        _mem_kv_kernel,
        grid=(rows // tm,),
        in_specs=[pl.BlockSpec((tm, D_MODEL), lambda i: (i, 0)),
                  _resident((1, D_MODEL)), _resident(w_kv.shape)],
        out_specs=pl.BlockSpec((tm, 2 * D_MODEL), lambda i: (i, 0)),
        out_shape=jax.ShapeDtypeStruct((rows, 2 * D_MODEL), jnp.bfloat16),
        compiler_params=_params(1),
        name="mem_kv_proj",
    )(mem2d, g_mem.reshape(1, -1), w_kv)


def kernel(x, mem, g_mix, w_in, conv_w, g_attn_out, g_conv_out, w_out, g_xattn, g_mem,
           w_q_mem, w_kv_mem, w_o_mem, g_mlp, w_up, w_down, g_final):
    b, seq, d = x.shape
    mem_len = mem.shape[1]
    assert tuple(dil for _, dil in DILATED_PATTERNS) == (1, 4, 16)
    assert all(win // dil == BAND_STEPS for win, dil in DILATED_PATTERNS)
    bf = lambda w: w.astype(jnp.bfloat16)
    x2d = x.reshape(b * seq, d)

    (q1, k1, v1, q4, k4, v4, q16, k16, v16, convn) = _in_proj(
        x2d, g_mix, bf(w_in), conv_w, g_conv_out, b, seq)
    as_stream = lambda a: a.reshape(b, 1, seq, ATT_WIDTH)
    o1, l1 = _dilated_attention(as_stream(q1), as_stream(k1), as_stream(v1))
    o4, l4 = _dilated_attention(q4, k4, v4)
    o16, l16 = _dilated_attention(q16, k16, v16)
    flat = lambda a: a.reshape(b * seq, a.shape[-1])
    kv = _mem_kv(mem.reshape(b * mem_len, d), g_mem, bf(w_kv_mem))
    y = _post_attn(flat(o1), flat(l1), o4, l4, o16, l16, convn, x2d, kv,
                   g_attn_out, bf(w_out), g_xattn, bf(w_q_mem), bf(w_o_mem),
                   g_mlp, bf(w_up), bf(w_down), g_final, seq, mem_len)
    return y.reshape(b, seq, d)
```

```python
import functools
import math

import jax
import jax.numpy as jnp
from jax.experimental import pallas as pl
from jax.experimental.pallas import tpu as pltpu

D_MODEL = 1024
ATT_WIDTH = 512
HEAD_DIM = 64
N_ATT_HEADS = 8
CONV_WIDTH = 512
CONV_K = 3
DILATED_PATTERNS = ((128, 1), (512, 4), (2048, 16))
BAND_STEPS = 128
N_MEM_HEADS = 4
MEM_HEAD_DIM = 256
D_FF = 4096
NORM_EPS = 1e-6
NEG_INF = -1e30
LOG2_E = math.log2(math.e)

LANES = 128
F32_SUBLANES = 8
BF16_SUBLANES = 16
PAIR_LANES = 2 * HEAD_DIM
assert PAIR_LANES == LANES
N_SLABS = ATT_WIDTH // LANES
LSE_REP = LANES // N_ATT_HEADS
VMEM_LIMIT_BYTES = 58 * 1024 * 1024

TOKEN_TILE = 1024
POST_TILE = 512
FF_CHUNK = 1024
MLP_CHUNKS_PER_STAGE = (2, 1, 1)
ATTN_SUBTILES = 16
RELAYOUT_RADIX = 4


def _rms(x, g):
    ms = jnp.mean(x * x, axis=-1, keepdims=True)
    return x * jax.lax.rsqrt(ms + NORM_EPS) * g


def _dot(a, b):
    return jnp.dot(a, b, preferred_element_type=jnp.float32)


def _dot_nt(a, b):
    return jax.lax.dot_general(a, b, (((1,), (1,)), ((), ())),
                               preferred_element_type=jnp.float32)


def _params(n_axes):
    return pltpu.CompilerParams(
        dimension_semantics=("arbitrary",) * n_axes,
        vmem_limit_bytes=VMEM_LIMIT_BYTES)


def _resident(shape):
    return pl.BlockSpec(shape, lambda *_: (0,) * len(shape), pipeline_mode=pl.Buffered(1))


def _slab(j):
    return slice(j * LANES, (j + 1) * LANES)


def _split_streams(nat_ref, out4_ref, out16_ref, by4_ref):
    tm = nat_ref.shape[1]
    n4, n16 = tm // 4, tm // 16
    for r in range(RELAYOUT_RADIX):
        for j in range(N_SLABS):
            piece = nat_ref[j, pl.ds(r, n4, stride=RELAYOUT_RADIX), :]
            by4_ref[j, r * n4:(r + 1) * n4, :] = piece
            out4_ref[r, :, _slab(j)] = piece.astype(jnp.bfloat16)
    for r4 in range(RELAYOUT_RADIX):
        for a in range(RELAYOUT_RADIX):
            for j in range(N_SLABS):
                piece = by4_ref[j, pl.ds(r4 * n4 + a, n16, stride=RELAYOUT_RADIX), :]
                out16_ref[RELAYOUT_RADIX * a + r4, :, _slab(j)] = piece.astype(jnp.bfloat16)


def _merge_streams4(in4_ref, nat_ref):
    n4 = in4_ref.shape[1]
    for r in range(RELAYOUT_RADIX):
        for j in range(nat_ref.shape[0]):
            nat_ref[j, pl.ds(r, n4, stride=RELAYOUT_RADIX), :] = in4_ref[r, :, _slab(j)]


def _merge_streams16(in16_ref, by4_ref, nat_ref):
    n16 = in16_ref.shape[1]
    n4 = RELAYOUT_RADIX * n16
    n_slabs = nat_ref.shape[0]
    for r4 in range(RELAYOUT_RADIX):
        for a in range(RELAYOUT_RADIX):
            for j in range(n_slabs):
                by4_ref[j, pl.ds(r4 * n4 + a, n16, stride=RELAYOUT_RADIX), :] = (
                    in16_ref[RELAYOUT_RADIX * a + r4, :, _slab(j)])
    for r4 in range(RELAYOUT_RADIX):
        for j in range(n_slabs):
            nat_ref[j, pl.ds(r4, n4, stride=RELAYOUT_RADIX), :] = (
                by4_ref[j, r4 * n4:(r4 + 1) * n4, :])


def _in_proj_kernel(tiles_per_seq, n_cast, x_ref, g_ref, w_ref, cw_ref, gc_ref, *refs):
    cast_in, refs = refs[:n_cast], refs[n_cast:]
    (q1_ref, k1_ref, v1_ref, q4_ref, k4_ref, v4_ref, q16_ref, k16_ref, v16_ref,
     c_ref) = refs[:10]
    cast_out, (ubuf_ref, nat_ref, by4_ref) = refs[10:10 + n_cast], refs[10 + n_cast:]
    tm = x_ref.shape[0]
    for src, dst in zip(cast_in, cast_out):
        dst[...] = src[...].astype(jnp.bfloat16)

    @pl.when(pl.program_id(0) == 0)
    def _():
        ubuf_ref[tm:tm + F32_SUBLANES, :] = jnp.zeros((F32_SUBLANES, CONV_WIDTH), jnp.float32)

    h = _rms(x_ref[...], g_ref[...]).astype(jnp.bfloat16)

    def proj(col):
        return _dot(h, w_ref[:, col * ATT_WIDTH:(col + 1) * ATT_WIDTH])

    def project_streams(col, out1_ref, out4_ref, out16_ref, scale=None):
        val = proj(col) if scale is None else proj(col) * scale
        out1_ref[...] = val.astype(jnp.bfloat16)
        for j in range(N_SLABS):
            nat_ref[j] = val[:, _slab(j)]
        _split_streams(nat_ref, out4_ref, out16_ref, by4_ref)

    project_streams(0, q1_ref, q4_ref, q16_ref, LOG2_E / math.sqrt(HEAD_DIM))
    project_streams(1, k1_ref, k4_ref, k16_ref)
    project_streams(2, v1_ref, v4_ref, v16_ref)

    seq_start = pl.program_id(0) % tiles_per_seq == 0
    ubuf_ref[0:F32_SUBLANES, :] = jnp.where(seq_start, 0.0, ubuf_ref[tm:tm + F32_SUBLANES, :])
    u = proj(4) * proj(5)
    ubuf_ref[F32_SUBLANES:tm + F32_SUBLANES, :] = u
    conv = (u * cw_ref[2:3, :]
            + ubuf_ref[F32_SUBLANES - 1:tm + F32_SUBLANES - 1, :] * cw_ref[1:2, :]
            + ubuf_ref[F32_SUBLANES - 2:tm + F32_SUBLANES - 2, :] * cw_ref[0:1, :])
    y = proj(3) * conv
    c_ref[...] = _rms(y, gc_ref[...]).astype(jnp.bfloat16)


def _stream_spec(dilation, tm, tiles_per_seq):
    return pl.BlockSpec((None, dilation, tm // dilation, ATT_WIDTH),
                        lambda i: (i // tiles_per_seq, 0, i % tiles_per_seq, 0))


def _in_proj(x2d, g_mix, w_in, conv_w, g_conv_out, later_weights, batch, seq_len):
    t = x2d.shape[0]
    tm = TOKEN_TILE
    assert seq_len % tm == 0 and t == batch * seq_len
    tiles_per_seq = seq_len // tm
    n_steps = t // tm
    nat = jax.ShapeDtypeStruct((t, ATT_WIDTH), jnp.bfloat16)
    by = lambda d: jax.ShapeDtypeStruct((batch, d, seq_len // d, ATT_WIDTH), jnp.bfloat16)
    row_spec = pl.BlockSpec((tm, ATT_WIDTH), lambda i: (i, 0))
    s4, s16 = _stream_spec(4, tm, tiles_per_seq), _stream_spec(16, tm, tiles_per_seq)
    assert all(w.shape[0] % (n_steps * BF16_SUBLANES) == 0 for w in later_weights)
    cast_specs = [pl.BlockSpec((w.shape[0] // n_steps, w.shape[1]), lambda i: (i, 0))
                  for w in later_weights]
    return pl.pallas_call(
        functools.partial(_in_proj_kernel, tiles_per_seq, len(later_weights)),
        grid=(n_steps,),
        in_specs=[
            pl.BlockSpec((tm, D_MODEL), lambda i: (i, 0)),
            _resident((1, D_MODEL)),
            _resident(w_in.shape),
            _resident(conv_w.shape),
            _resident((1, CONV_WIDTH)),
        ] + cast_specs,
        out_specs=[row_spec] * 3 + [s4] * 3 + [s16] * 3 + [row_spec] + cast_specs,
        out_shape=([nat] * 3 + [by(4)] * 3 + [by(16)] * 3 + [nat]
                   + [jax.ShapeDtypeStruct(w.shape, jnp.bfloat16) for w in later_weights]),
        scratch_shapes=[pltpu.VMEM((tm + F32_SUBLANES, CONV_WIDTH), jnp.float32),
                        pltpu.VMEM((N_SLABS, tm, LANES), jnp.float32),
                        pltpu.VMEM((N_SLABS, tm, LANES), jnp.float32)],
        compiler_params=_params(1),
        name="in_proj",
    )(x2d, g_mix.reshape(1, -1), w_in, conv_w, g_conv_out.reshape(1, -1), *later_weights)


def _attn_kernel(q_ref, kc_ref, kp_ref, vc_ref, vp_ref, o_ref, lse_ref):
    s = BAND_STEPS
    streams, row_blocks = q_ref.shape[0], q_ref.shape[1] // s
    has_prev = pl.program_id(2) > 0
    qi = jax.lax.broadcasted_iota(jnp.int32, (2 * s, 2 * s), 0) & (s - 1)
    kj = jax.lax.broadcasted_iota(jnp.int32, (2 * s, 2 * s), 1)
    band = (kj >= qi) & (kj <= qi + s)
    cap = jnp.where(band, jnp.inf, NEG_INF)
    cap_first = jnp.where(band & ((kj >= s) | has_prev), jnp.inf, NEG_INF)
    first_head = jax.lax.broadcasted_iota(jnp.int32, (s, PAIR_LANES), 1) < HEAD_DIM
    ones = jnp.ones((2 * s, PAIR_LANES), jnp.bfloat16)

    def rows_of(i):
        return slice(i * s, (i + 1) * s)

    def scores(r, i, pair):
        lanes = _slab(pair)
        q2 = q_ref[r, rows_of(i), lanes]
        zero = jnp.zeros_like(q2)
        q = jnp.concatenate([jnp.where(first_head, q2, zero),
                             jnp.where(first_head, zero, q2)], axis=0)
        k_prev = kp_ref[r, :, lanes] if i == 0 else kc_ref[r, rows_of(i - 1), lanes]
        k = jnp.concatenate([k_prev, kc_ref[r, rows_of(i), lanes]], axis=0)
        return jnp.minimum(_dot_nt(q, k), cap_first if i == 0 else cap)

    def finish(r, i, pair, sc):
        lanes = _slab(pair)
        v_prev = vp_ref[r, :, lanes] if i == 0 else vc_ref[r, rows_of(i - 1), lanes]
        v = jnp.concatenate(
            [jnp.concatenate([v_prev, vc_ref[r, rows_of(i), lanes]], axis=0), ones], axis=1)
        m = jnp.max(sc, axis=-1, keepdims=True)
        p = jnp.exp2(sc - m)
        pv = _dot(p.astype(jnp.bfloat16), v)
        l = pv[:, PAIR_LANES:]
        o = pv[:, :PAIR_LANES] / l
        o_ref[r, rows_of(i), lanes] = jnp.where(first_head, o[:s], o[s:])
        lse = m + jnp.log2(l)
        for half in range(2):
            at = slice((2 * pair + half) * LSE_REP, (2 * pair + half + 1) * LSE_REP)
            lse_ref[r, rows_of(i), at] = lse[half * s:(half + 1) * s, at]

    items = [(r, i, pair) for r in range(streams) for i in range(row_blocks)
             for pair in range(N_ATT_HEADS // 2)]
    sc = scores(*items[0])
    for n, item in enumerate(items):
        sc_next = scores(*items[n + 1]) if n + 1 < len(items) else None
        finish(*item, sc)
        sc = sc_next


def _dilated_attention(q, k, v):
    b, d, steps, _ = q.shape
    row_blocks = min(ATTN_SUBTILES, steps // BAND_STEPS)
    streams = ATTN_SUBTILES // row_blocks
    tile_rows = row_blocks * BAND_STEPS
    assert steps % tile_rows == 0 and d % streams == 0
    cur = pl.BlockSpec((None, streams, tile_rows, ATT_WIDTH), lambda bi, g, n: (bi, g, n, 0))
    prev = pl.BlockSpec((None, streams, BAND_STEPS, ATT_WIDTH),
                        lambda bi, g, n: (bi, g, jnp.maximum(n * row_blocks - 1, 0), 0))
    cur_lse = pl.BlockSpec((None, streams, tile_rows, LANES), lambda bi, g, n: (bi, g, n, 0))
    return pl.pallas_call(
        _attn_kernel,
        grid=(b, d // streams, steps // tile_rows),
        in_specs=[cur, cur, prev, cur, prev],
        out_specs=[cur, cur_lse],
        out_shape=[jax.ShapeDtypeStruct(q.shape, jnp.float32),
                   jax.ShapeDtypeStruct((b, d, steps, LANES), jnp.float32)],
        compiler_params=_params(3),
        name=f"dilated_attn_d{d}",
    )(q, k, k, v, v)


def _post_attn_kernel(o1_ref, l1_ref, o4_ref, l4_ref, o16_ref, l16_ref, c_ref, x_ref,
                      ga_ref, wout_ref, gx_ref, wq_ref, kv_ref, wo_ref,
                      gm_ref, wu_ref, wd_ref, gf_ref, y_ref,
                      on4_ref, ln4_ref, on16_ref, ln16_ref, by4_ref, xmid_ref):
    i = pl.program_id(0)

    @pl.when(i == 0)
    def _():
        xmid_ref[...] = jnp.zeros(xmid_ref.shape, jnp.float32)

    assert sum(MLP_CHUNKS_PER_STAGE) * FF_CHUNK == D_FF
    y = xmid_ref[(i + 1) % 2]
    h_mlp = _rms(y, gm_ref[...]).astype(jnp.bfloat16)
    chunk_ids = iter(range(D_FF // FF_CHUNK))

    def mlp_stage(y, stage):
        for _ in range(MLP_CHUNKS_PER_STAGE[stage]):
            c = next(chunk_ids)
            cols = slice(c * FF_CHUNK, (c + 1) * FF_CHUNK)
            a = jnp.maximum(_dot(h_mlp, wu_ref[:, cols]), 0.0)
            y = y + _dot((a * a).astype(jnp.bfloat16), wd_ref[cols, :])
        return y

    _merge_streams4(o4_ref, on4_ref)
    _merge_streams4(l4_ref, ln4_ref)
    _merge_streams16(o16_ref, by4_ref, on16_ref)
    _merge_streams16(l16_ref, by4_ref, ln16_ref)
    lses = [l1_ref[...], ln4_ref[0], ln16_ref[0]]
    top = jnp.maximum(jnp.maximum(lses[0], lses[1]), lses[2])
    es = [jnp.exp2(l - top) for l in lses]
    inv = 1.0 / (es[0] + es[1] + es[2])
    weights = [e * inv for e in es]
    lane = jax.lax.broadcasted_iota(jnp.int32, weights[0].shape, 1)
    slabs = []
    for j in range(N_SLABS):
        src = (2 * j + lane // HEAD_DIM) * LSE_REP
        w = [jnp.take_along_axis(wp, src, axis=1) for wp in weights]
        slabs.append(w[0] * o1_ref[:, _slab(j)] + w[1] * on4_ref[j] + w[2] * on16_ref[j])
    attn = jnp.concatenate(slabs, axis=-1)
    attn_n = _rms(attn, ga_ref[...]).astype(jnp.bfloat16)
    y = mlp_stage(y, 0)
    x = (x_ref[...] + _dot(attn_n, wout_ref[0:ATT_WIDTH, :])
         + _dot(c_ref[...], wout_ref[ATT_WIDTH:, :]))

    y = mlp_stage(y, 1)

    h = _rms(x, gx_ref[...]).astype(jnp.bfloat16)
    q = (_dot(h, wq_ref[...]) * (LOG2_E / math.sqrt(MEM_HEAD_DIM))).astype(jnp.bfloat16)
    head = lambda hd: slice(hd * MEM_HEAD_DIM, (hd + 1) * MEM_HEAD_DIM)
    scores = [_dot_nt(q[:, head(hd)], kv_ref[:, head(hd)]) for hd in range(N_MEM_HEADS)]

    y = mlp_stage(y, 2)
    y_ref[...] = _rms(y, gf_ref[...])

    heads = []
    for hd in range(N_MEM_HEADS):
        sc = scores[hd]
        m = jnp.max(sc, axis=-1, keepdims=True)
        p = jnp.exp2(sc - m)
        l = jnp.sum(p, axis=-1, keepdims=True)
        v = kv_ref[:, D_MODEL + hd * MEM_HEAD_DIM:D_MODEL + (hd + 1) * MEM_HEAD_DIM]
        heads.append((_dot(p.astype(jnp.bfloat16), v) / l).astype(jnp.bfloat16))
    xmid_ref[i % 2] = x + _dot(jnp.concatenate(heads, axis=-1), wo_ref[...])


def _post_attn(o1, l1, o4, l4, o16, l16, convn, x2d, kv, g_attn_out, w_out, g_xattn, w_q, w_o,
               g_mlp, w_up, w_down, g_final, seq_len, mem_len):
    t = x2d.shape[0]
    tm = POST_TILE
    assert seq_len % tm == 0
    tiles_per_seq = seq_len // tm
    n_tiles = t // tm
    tile_in = lambda i: jnp.minimum(i, n_tiles - 1)
    row_spec = lambda width: pl.BlockSpec((tm, width), lambda i: (tile_in(i), 0))
    stream_spec = lambda d, width: pl.BlockSpec(
        (None, d, tm // d, width),
        lambda i: (tile_in(i) // tiles_per_seq, 0, tile_in(i) % tiles_per_seq, 0))
    x_spec = pl.BlockSpec((tm, D_MODEL), lambda i: (tile_in(i), 0))
    gain = lambda g: g.reshape(1, -1)
    nat = pltpu.VMEM((N_SLABS, tm, LANES), jnp.float32)
    nat_lse = pltpu.VMEM((1, tm, LANES), jnp.float32)
    return pl.pallas_call(
        _post_attn_kernel,
        grid=(n_tiles + 1,),
        in_specs=[row_spec(ATT_WIDTH), row_spec(LANES),
                  stream_spec(4, ATT_WIDTH), stream_spec(4, LANES),
                  stream_spec(16, ATT_WIDTH), stream_spec(16, LANES),
                  row_spec(ATT_WIDTH), x_spec,
                  _resident((1, ATT_WIDTH)), _resident(w_out.shape),
                  _resident((1, D_MODEL)), _resident(w_q.shape),
                  pl.BlockSpec((mem_len, 2 * D_MODEL),
                               lambda i: (tile_in(i) // tiles_per_seq, 0)),
                  _resident(w_o.shape),
                  _resident((1, D_MODEL)), _resident(w_up.shape), _resident(w_down.shape),
                  _resident((1, D_MODEL))],
        out_specs=pl.BlockSpec((tm, D_MODEL), lambda i: (jnp.maximum(i - 1, 0), 0)),
        out_shape=jax.ShapeDtypeStruct((t, D_MODEL), jnp.float32),
        scratch_shapes=[nat, nat_lse, nat, nat_lse, nat,
                        pltpu.VMEM((2, tm, D_MODEL), jnp.float32)],
        compiler_params=_params(1),
        name="post_attn",
    )(o1, l1, o4, l4, o16, l16, convn, x2d, gain(g_attn_out), w_out, gain(g_xattn), w_q, kv, w_o,
      gain(g_mlp), w_up, w_down, gain(g_final))


def _mem_kv_kernel(m_ref, g_ref, w_ref, kv_ref):
    h = _rms(m_ref[...], g_ref[...]).astype(jnp.bfloat16)
    kv_ref[...] = _dot(h, w_ref[...]).astype(jnp.bfloat16)


def _mem_kv(mem2d, g_mem, w_kv):
    rows = mem2d.shape[0]
    tm = TOKEN_TILE
    return pl.pallas_call(
        _mem_kv_kernel,
        grid=(rows // tm,),
        in_specs=[pl.BlockSpec((tm, D_MODEL), lambda i: (i, 0)),
                  _resident((1, D_MODEL)), _resident(w_kv.shape)],
        out_specs=pl.BlockSpec((tm, 2 * D_MODEL), lambda i: (i, 0)),
        out_shape=jax.ShapeDtypeStruct((rows, 2 * D_MODEL), jnp.bfloat16),
        compiler_params=_params(1),
        name="mem_kv_proj",
    )(mem2d, g_mem.reshape(1, -1), w_kv)


def kernel(x, mem, g_mix, w_in, conv_w, g_attn_out, g_conv_out, w_out, g_xattn, g_mem,
           w_q_mem, w_kv_mem, w_o_mem, g_mlp, w_up, w_down, g_final):
    b, seq, d = x.shape
    mem_len = mem.shape[1]
    assert tuple(dil for _, dil in DILATED_PATTERNS) == (1, 4, 16)
    assert all(win // dil == BAND_STEPS for win, dil in DILATED_PATTERNS)
    bf = lambda w: w.astype(jnp.bfloat16)
    x2d = x.reshape(b * seq, d)

    (q1, k1, v1, q4, k4, v4, q16, k16, v16, convn,
     w_out_bf, w_q_bf, w_o_bf, w_up_bf, w_down_bf) = _in_proj(
        x2d, g_mix, bf(w_in), conv_w, g_conv_out,
        (w_out, w_q_mem, w_o_mem, w_up, w_down), b, seq)
    as_stream = lambda a: a.reshape(b, 1, seq, ATT_WIDTH)
    o1, l1 = _dilated_attention(as_stream(q1), as_stream(k1), as_stream(v1))
    o4, l4 = _dilated_attention(q4, k4, v4)
    o16, l16 = _dilated_attention(q16, k16, v16)
    flat = lambda a: a.reshape(b * seq, a.shape[-1])
    kv = _mem_kv(mem.reshape(b * mem_len, d), g_mem, bf(w_kv_mem))
    y = _post_attn(flat(o1), flat(l1), o4, l4, o16, l16, convn, x2d, kv,
                   g_attn_out, w_out_bf, g_xattn, w_q_bf, w_o_bf,
                   g_mlp, w_up_bf, w_down_bf, g_final, seq, mem_len)
    return y.reshape(b, seq, d)
```

```python
import functools
import math

import jax
import jax.numpy as jnp
from jax.experimental import pallas as pl
from jax.experimental.pallas import tpu as pltpu

D_MODEL = 1024
ATT_WIDTH = 512
HEAD_DIM = 64
N_ATT_HEADS = 8
CONV_WIDTH = 512
CONV_K = 3
DILATED_PATTERNS = ((128, 1), (512, 4), (2048, 16))
BAND_STEPS = 128
N_MEM_HEADS = 4
MEM_HEAD_DIM = 256
D_FF = 4096
NORM_EPS = 1e-6
NEG_INF = -1e30
LOG2_E = math.log2(math.e)

LANES = 128
F32_SUBLANES = 8
BF16_SUBLANES = 16
PAIR_LANES = 2 * HEAD_DIM
assert PAIR_LANES == LANES
N_SLABS = ATT_WIDTH // LANES
LSE_REP = LANES // N_ATT_HEADS
VMEM_LIMIT_BYTES = 58 * 1024 * 1024

TOKEN_TILE = 1024
POST_TILE = 512
FF_CHUNK = 1024
MLP_CHUNKS_PER_STAGE = (2, 1, 1)
ATTN_SUBTILES = 16
RELAYOUT_RADIX = 4


def _rms(x, g):
    ms = jnp.mean(x * x, axis=-1, keepdims=True)
    return x * jax.lax.rsqrt(ms + NORM_EPS) * g


def _dot(a, b):
    return jnp.dot(a, b, preferred_element_type=jnp.float32)


def _dot_nt(a, b):
    return jax.lax.dot_general(a, b, (((1,), (1,)), ((), ())),
                               preferred_element_type=jnp.float32)


def _params(n_axes):
    return pltpu.CompilerParams(
        dimension_semantics=("arbitrary",) * n_axes,
        vmem_limit_bytes=VMEM_LIMIT_BYTES)


def _resident(shape):
    return pl.BlockSpec(shape, lambda *_: (0,) * len(shape), pipeline_mode=pl.Buffered(1))


def _slab(j):
    return slice(j * LANES, (j + 1) * LANES)


def _split_streams(nat_ref, out4_ref, out16_ref, by4_ref):
    tm = nat_ref.shape[1]
    n4, n16 = tm // 4, tm // 16
    for r in range(RELAYOUT_RADIX):
        for j in range(N_SLABS):
            piece = nat_ref[j, pl.ds(r, n4, stride=RELAYOUT_RADIX), :]
            by4_ref[j, r * n4:(r + 1) * n4, :] = piece
            out4_ref[r, :, _slab(j)] = piece.astype(jnp.bfloat16)
    for r4 in range(RELAYOUT_RADIX):
        for a in range(RELAYOUT_RADIX):
            for j in range(N_SLABS):
                piece = by4_ref[j, pl.ds(r4 * n4 + a, n16, stride=RELAYOUT_RADIX), :]
                out16_ref[RELAYOUT_RADIX * a + r4, :, _slab(j)] = piece.astype(jnp.bfloat16)


def _merge_streams4(in4_ref, nat_ref):
    n4 = in4_ref.shape[1]
    for r in range(RELAYOUT_RADIX):
        for j in range(nat_ref.shape[0]):
            nat_ref[j, pl.ds(r, n4, stride=RELAYOUT_RADIX), :] = in4_ref[r, :, _slab(j)]


def _merge_streams16(in16_ref, by4_ref, nat_ref):
    n16 = in16_ref.shape[1]
    n4 = RELAYOUT_RADIX * n16
    n_slabs = nat_ref.shape[0]
    for r4 in range(RELAYOUT_RADIX):
        for a in range(RELAYOUT_RADIX):
            for j in range(n_slabs):
                by4_ref[j, pl.ds(r4 * n4 + a, n16, stride=RELAYOUT_RADIX), :] = (
                    in16_ref[RELAYOUT_RADIX * a + r4, :, _slab(j)])
    for r4 in range(RELAYOUT_RADIX):
        for j in range(n_slabs):
            nat_ref[j, pl.ds(r4, n4, stride=RELAYOUT_RADIX), :] = (
                by4_ref[j, r4 * n4:(r4 + 1) * n4, :])


def _in_proj_kernel(tiles_per_seq, n_cast, x_ref, g_ref, w_ref, cw_ref, gc_ref, *refs):
    cast_in, refs = refs[:n_cast], refs[n_cast:]
    (q1_ref, k1_ref, v1_ref, q4_ref, k4_ref, v4_ref, q16_ref, k16_ref, v16_ref,
     c_ref) = refs[:10]
    cast_out, (ubuf_ref, nat_ref, by4_ref) = refs[10:10 + n_cast], refs[10 + n_cast:]
    tm = x_ref.shape[0]
    for src, dst in zip(cast_in, cast_out):
        dst[...] = src[...].astype(jnp.bfloat16)

    @pl.when(pl.program_id(0) == 0)
    def _():
        ubuf_ref[tm:tm + F32_SUBLANES, :] = jnp.zeros((F32_SUBLANES, CONV_WIDTH), jnp.float32)

    h = _rms(x_ref[...], g_ref[...]).astype(jnp.bfloat16)

    def proj(col):
        return _dot(h, w_ref[:, col * ATT_WIDTH:(col + 1) * ATT_WIDTH])

    def project_streams(col, out1_ref, out4_ref, out16_ref, scale=None):
        val = proj(col) if scale is None else proj(col) * scale
        out1_ref[...] = val.astype(jnp.bfloat16)
        for j in range(N_SLABS):
            nat_ref[j] = val[:, _slab(j)]
        _split_streams(nat_ref, out4_ref, out16_ref, by4_ref)

    project_streams(0, q1_ref, q4_ref, q16_ref, LOG2_E / math.sqrt(HEAD_DIM))
    project_streams(1, k1_ref, k4_ref, k16_ref)
    project_streams(2, v1_ref, v4_ref, v16_ref)

    seq_start = pl.program_id(0) % tiles_per_seq == 0
    ubuf_ref[0:F32_SUBLANES, :] = jnp.where(seq_start, 0.0, ubuf_ref[tm:tm + F32_SUBLANES, :])
    u = proj(4) * proj(5)
    ubuf_ref[F32_SUBLANES:tm + F32_SUBLANES, :] = u
    conv = (u * cw_ref[2:3, :]
            + ubuf_ref[F32_SUBLANES - 1:tm + F32_SUBLANES - 1, :] * cw_ref[1:2, :]
            + ubuf_ref[F32_SUBLANES - 2:tm + F32_SUBLANES - 2, :] * cw_ref[0:1, :])
    y = proj(3) * conv
    c_ref[...] = _rms(y, gc_ref[...]).astype(jnp.bfloat16)


def _stream_spec(dilation, tm, tiles_per_seq):
    return pl.BlockSpec((None, dilation, tm // dilation, ATT_WIDTH),
                        lambda i: (i // tiles_per_seq, 0, i % tiles_per_seq, 0))


def _in_proj(x2d, g_mix, w_in, conv_w, g_conv_out, later_weights, batch, seq_len):
    t = x2d.shape[0]
    tm = TOKEN_TILE
    assert seq_len % tm == 0 and t == batch * seq_len
    tiles_per_seq = seq_len // tm
    n_steps = t // tm
    nat = jax.ShapeDtypeStruct((t, ATT_WIDTH), jnp.bfloat16)
    by = lambda d: jax.ShapeDtypeStruct((batch, d, seq_len // d, ATT_WIDTH), jnp.bfloat16)
    row_spec = pl.BlockSpec((tm, ATT_WIDTH), lambda i: (i, 0))
    s4, s16 = _stream_spec(4, tm, tiles_per_seq), _stream_spec(16, tm, tiles_per_seq)
    assert all(w.shape[0] % (n_steps * BF16_SUBLANES) == 0 for w in later_weights)
    cast_specs = [pl.BlockSpec((w.shape[0] // n_steps, w.shape[1]), lambda i: (i, 0))
                  for w in later_weights]
    return pl.pallas_call(
        functools.partial(_in_proj_kernel, tiles_per_seq, len(later_weights)),
        grid=(n_steps,),
        in_specs=[
            pl.BlockSpec((tm, D_MODEL), lambda i: (i, 0)),
            _resident((1, D_MODEL)),
            _resident(w_in.shape),
            _resident(conv_w.shape),
            _resident((1, CONV_WIDTH)),
        ] + cast_specs,
        out_specs=[row_spec] * 3 + [s4] * 3 + [s16] * 3 + [row_spec] + cast_specs,
        out_shape=([nat] * 3 + [by(4)] * 3 + [by(16)] * 3 + [nat]
                   + [jax.ShapeDtypeStruct(w.shape, jnp.bfloat16) for w in later_weights]),
        scratch_shapes=[pltpu.VMEM((tm + F32_SUBLANES, CONV_WIDTH), jnp.float32),
                        pltpu.VMEM((N_SLABS, tm, LANES), jnp.float32),
                        pltpu.VMEM((N_SLABS, tm, LANES), jnp.float32)],
        compiler_params=_params(1),
        name="in_proj",
    )(x2d, g_mix.reshape(1, -1), w_in, conv_w, g_conv_out.reshape(1, -1), *later_weights)


def _attn_kernel(q_ref, kc_ref, kp_ref, vc_ref, vp_ref, o_ref, lse_ref):
    s = BAND_STEPS
    streams, row_blocks = q_ref.shape[0], q_ref.shape[1] // s
    has_prev = pl.program_id(2) > 0
    qi = jax.lax.broadcasted_iota(jnp.int32, (2 * s, 2 * s), 0) & (s - 1)
    kj = jax.lax.broadcasted_iota(jnp.int32, (2 * s, 2 * s), 1)
    band = (kj >= qi) & (kj <= qi + s)
    cap = jnp.where(band, jnp.inf, NEG_INF)
    cap_first = jnp.where(band & ((kj >= s) | has_prev), jnp.inf, NEG_INF)
    first_head = jax.lax.broadcasted_iota(jnp.int32, (s, PAIR_LANES), 1) < HEAD_DIM
    ones = jnp.ones((2 * s, PAIR_LANES), jnp.bfloat16)

    def rows_of(i):
        return slice(i * s, (i + 1) * s)

    def scores(r, i, pair):
        lanes = _slab(pair)
        q2 = q_ref[r, rows_of(i), lanes]
        zero = jnp.zeros_like(q2)
        q = jnp.concatenate([jnp.where(first_head, q2, zero),
                             jnp.where(first_head, zero, q2)], axis=0)
        k_prev = kp_ref[r, :, lanes] if i == 0 else kc_ref[r, rows_of(i - 1), lanes]
        k = jnp.concatenate([k_prev, kc_ref[r, rows_of(i), lanes]], axis=0)
        return jnp.minimum(_dot_nt(q, k), cap_first if i == 0 else cap)

    def finish(r, i, pair, sc):
        lanes = _slab(pair)
        v_prev = vp_ref[r, :, lanes] if i == 0 else vc_ref[r, rows_of(i - 1), lanes]
        v = jnp.concatenate(
            [jnp.concatenate([v_prev, vc_ref[r, rows_of(i), lanes]], axis=0), ones], axis=1)
        m = jnp.max(sc, axis=-1, keepdims=True)
        p = jnp.exp2(sc - m)
        pv = _dot(p.astype(jnp.bfloat16), v)
        l = pv[:, PAIR_LANES:]
        o = pv[:, :PAIR_LANES] / l
        o_ref[r, rows_of(i), lanes] = jnp.where(first_head, o[:s], o[s:])
        lse = m + jnp.log2(l)
        for half in range(2):
            at = slice((2 * pair + half) * LSE_REP, (2 * pair + half + 1) * LSE_REP)
            lse_ref[r, rows_of(i), at] = lse[half * s:(half + 1) * s, at]

    items = [(r, i, pair) for r in range(streams) for i in range(row_blocks)
             for pair in range(N_ATT_HEADS // 2)]
    sc = scores(*items[0])
    for n, item in enumerate(items):
        sc_next = scores(*items[n + 1]) if n + 1 < len(items) else None
        finish(*item, sc)
        sc = sc_next


def _dilated_attention(q, k, v):
    b, d, steps, _ = q.shape
    row_blocks = min(ATTN_SUBTILES, steps // BAND_STEPS)
    streams = ATTN_SUBTILES // row_blocks
    tile_rows = row_blocks * BAND_STEPS
    assert steps % tile_rows == 0 and d % streams == 0
    cur = pl.BlockSpec((None, streams, tile_rows, ATT_WIDTH), lambda bi, g, n: (bi, g, n, 0))
    prev = pl.BlockSpec((None, streams, BAND_STEPS, ATT_WIDTH),
                        lambda bi, g, n: (bi, g, jnp.maximum(n * row_blocks - 1, 0), 0))
    cur_lse = pl.BlockSpec((None, streams, tile_rows, LANES), lambda bi, g, n: (bi, g, n, 0))
    return pl.pallas_call(
        _attn_kernel,
        grid=(b, d // streams, steps // tile_rows),
        in_specs=[cur, cur, prev, cur, prev],
        out_specs=[cur, cur_lse],
        out_shape=[jax.ShapeDtypeStruct(q.shape, jnp.float32),
                   jax.ShapeDtypeStruct((b, d, steps, LANES), jnp.float32)],
        compiler_params=_params(3),
        name=f"dilated_attn_d{d}",
    )(q, k, k, v, v)


def _post_attn_kernel(o1_ref, l1_ref, o4_ref, l4_ref, o16_ref, l16_ref, c_ref, x_ref,
                      ga_ref, wout_ref, gx_ref, wq_ref, kv_ref, wo_ref,
                      gm_ref, wu_ref, wd_ref, gf_ref, y_ref,
                      on4_ref, ln4_ref, on16_ref, ln16_ref, by4_ref, xmid_ref):
    i = pl.program_id(0)

    @pl.when(i == 0)
    def _():
        xmid_ref[...] = jnp.zeros(xmid_ref.shape, jnp.float32)

    assert sum(MLP_CHUNKS_PER_STAGE) * FF_CHUNK == D_FF
    y = xmid_ref[(i + 1) % 2]
    h_mlp = _rms(y, gm_ref[...]).astype(jnp.bfloat16)
    chunk_ids = iter(range(D_FF // FF_CHUNK))

    def mlp_stage(y, stage):
        for _ in range(MLP_CHUNKS_PER_STAGE[stage]):
            c = next(chunk_ids)
            cols = slice(c * FF_CHUNK, (c + 1) * FF_CHUNK)
            a = jnp.maximum(_dot(h_mlp, wu_ref[:, cols]), 0.0)
            y = y + _dot((a * a).astype(jnp.bfloat16), wd_ref[cols, :])
        return y

    _merge_streams4(o4_ref, on4_ref)
    _merge_streams4(l4_ref, ln4_ref)
    _merge_streams16(o16_ref, by4_ref, on16_ref)
    _merge_streams16(l16_ref, by4_ref, ln16_ref)
    lses = [l1_ref[...], ln4_ref[0], ln16_ref[0]]
    top = jnp.maximum(jnp.maximum(lses[0], lses[1]), lses[2])
    es = [jnp.exp2(l - top) for l in lses]
    inv = 1.0 / (es[0] + es[1] + es[2])
    weights = [e * inv for e in es]
    lane = jax.lax.broadcasted_iota(jnp.int32, weights[0].shape, 1)
    slabs = []
    for j in range(N_SLABS):
        src = (2 * j + lane // HEAD_DIM) * LSE_REP
        w = [jnp.take_along_axis(wp, src, axis=1) for wp in weights]
        slabs.append(w[0] * o1_ref[:, _slab(j)] + w[1] * on4_ref[j] + w[2] * on16_ref[j])
    attn = jnp.concatenate(slabs, axis=-1)
    attn_n = _rms(attn, ga_ref[...]).astype(jnp.bfloat16)
    y = mlp_stage(y, 0)
    x = (x_ref[...] + _dot(attn_n, wout_ref[0:ATT_WIDTH, :])
         + _dot(c_ref[...], wout_ref[ATT_WIDTH:, :]))

    y = mlp_stage(y, 1)

    h = _rms(x, gx_ref[...]).astype(jnp.bfloat16)
    q = (_dot(h, wq_ref[...]) * (LOG2_E / math.sqrt(MEM_HEAD_DIM))).astype(jnp.bfloat16)
    head = lambda hd: slice(hd * MEM_HEAD_DIM, (hd + 1) * MEM_HEAD_DIM)
    scores = [_dot_nt(q[:, head(hd)], kv_ref[:, head(hd)]) for hd in range(N_MEM_HEADS)]

    y = mlp_stage(y, 2)
    y_ref[...] = _rms(y, gf_ref[...])

    heads = []
    for hd in range(N_MEM_HEADS):
        sc = scores[hd]
        m = jnp.max(sc, axis=-1, keepdims=True)
        p = jnp.exp2(sc - m)
        l = jnp.sum(p, axis=-1, keepdims=True)
        v = kv_ref[:, D_MODEL + hd * MEM_HEAD_DIM:D_MODEL + (hd + 1) * MEM_HEAD_DIM]
        heads.append((_dot(p.astype(jnp.bfloat16), v) / l).astype(jnp.bfloat16))
    xmid_ref[i % 2] = x + _dot(jnp.concatenate(heads, axis=-1), wo_ref[...])


def _post_attn(o1, l1, o4, l4, o16, l16, convn, x2d, kv, g_attn_out, w_out, g_xattn, w_q, w_o,
               g_mlp, w_up, w_down, g_final, seq_len, mem_len):
    t = x2d.shape[0]
    tm = POST_TILE
    assert seq_len % tm == 0
    tiles_per_seq = seq_len // tm
    n_tiles = t // tm
    tile_in = lambda i: jnp.minimum(i, n_tiles - 1)
    row_spec = lambda width: pl.BlockSpec((tm, width), lambda i: (tile_in(i), 0))
    stream_spec = lambda d, width: pl.BlockSpec(
        (None, d, tm // d, width),
        lambda i: (tile_in(i) // tiles_per_seq, 0, tile_in(i) % tiles_per_seq, 0))
    x_spec = pl.BlockSpec((tm, D_MODEL), lambda i: (tile_in(i), 0))
    gain = lambda g: g.reshape(1, -1)
    nat = pltpu.VMEM((N_SLABS, tm, LANES), jnp.float32)
    nat_lse = pltpu.VMEM((1, tm, LANES), jnp.float32)
    return pl.pallas_call(
        _post_attn_kernel,
        grid=(n_tiles + 1,),
        in_specs=[row_spec(ATT_WIDTH), row_spec(LANES),
                  stream_spec(4, ATT_WIDTH), stream_spec(4, LANES),
                  stream_spec(16, ATT_WIDTH), stream_spec(16, LANES),
                  row_spec(ATT_WIDTH), x_spec,
                  _resident((1, ATT_WIDTH)), _resident(w_out.shape),
                  _resident((1, D_MODEL)), _resident(w_q.shape),
                  pl.BlockSpec((mem_len, 2 * D_MODEL),
                               lambda i: (tile_in(i) // tiles_per_seq, 0)),
                  _resident(w_o.shape),
                  _resident((1, D_MODEL)), _resident(w_up.shape), _resident(w_down.shape),
                  _resident((1, D_MODEL))],
        out_specs=pl.BlockSpec((tm, D_MODEL), lambda i: (jnp.maximum(i - 1, 0), 0)),
        out_shape=jax.ShapeDtypeStruct((t, D_MODEL), jnp.float32),
        scratch_shapes=[nat, nat_lse, nat, nat_lse, nat,
                        pltpu.VMEM((2, tm, D_MODEL), jnp.float32)],
        compiler_params=_params(1),
        name="post_attn",
    )(o1, l1, o4, l4, o16, l16, convn, x2d, gain(g_attn_out), w_out, gain(g_xattn), w_q, kv, w_o,
      gain(g_mlp), w_up, w_down, gain(g_final))


def _mem_kv_kernel(m_ref, g_ref, w_ref, win_ref, kv_ref, win_bf_ref):
    h = _rms(m_ref[...], g_ref[...]).astype(jnp.bfloat16)
    kv_ref[...] = _dot(h, w_ref[...].astype(jnp.bfloat16)).astype(jnp.bfloat16)
    win_bf_ref[...] = win_ref[...].astype(jnp.bfloat16)


def _mem_kv(mem2d, g_mem, w_kv, w_in):
    rows = mem2d.shape[0]
    tm = TOKEN_TILE
    n_steps = rows // tm
    assert w_in.shape[0] % (n_steps * BF16_SUBLANES) == 0
    win_spec = pl.BlockSpec((w_in.shape[0] // n_steps, w_in.shape[1]), lambda i: (i, 0))
    return pl.pallas_call(
        _mem_kv_kernel,
        grid=(n_steps,),
        in_specs=[pl.BlockSpec((tm, D_MODEL), lambda i: (i, 0)),
                  _resident((1, D_MODEL)), _resident(w_kv.shape), win_spec],
        out_specs=[pl.BlockSpec((tm, 2 * D_MODEL), lambda i: (i, 0)), win_spec],
        out_shape=[jax.ShapeDtypeStruct((rows, 2 * D_MODEL), jnp.bfloat16),
                   jax.ShapeDtypeStruct(w_in.shape, jnp.bfloat16)],
        compiler_params=_params(1),
        name="mem_kv_proj",
    )(mem2d, g_mem.reshape(1, -1), w_kv, w_in)


def kernel(x, mem, g_mix, w_in, conv_w, g_attn_out, g_conv_out, w_out, g_xattn, g_mem,
           w_q_mem, w_kv_mem, w_o_mem, g_mlp, w_up, w_down, g_final):
    b, seq, d = x.shape
    mem_len = mem.shape[1]
    assert tuple(dil for _, dil in DILATED_PATTERNS) == (1, 4, 16)
    assert all(win // dil == BAND_STEPS for win, dil in DILATED_PATTERNS)
    x2d = x.reshape(b * seq, d)

    kv, w_in_bf = _mem_kv(mem.reshape(b * mem_len, d), g_mem, w_kv_mem, w_in)
    (q1, k1, v1, q4, k4, v4, q16, k16, v16, convn,
     w_out_bf, w_q_bf, w_o_bf, w_up_bf, w_down_bf) = _in_proj(
        x2d, g_mix, w_in_bf, conv_w, g_conv_out,
        (w_out, w_q_mem, w_o_mem, w_up, w_down), b, seq)
    as_stream = lambda a: a.reshape(b, 1, seq, ATT_WIDTH)
    o1, l1 = _dilated_attention(as_stream(q1), as_stream(k1), as_stream(v1))
    o4, l4 = _dilated_attention(q4, k4, v4)
    o16, l16 = _dilated_attention(q16, k16, v16)
    flat = lambda a: a.reshape(b * seq, a.shape[-1])
    y = _post_attn(flat(o1), flat(l1), o4, l4, o16, l16, convn, x2d, kv,
                   g_attn_out, w_out_bf, g_xattn, w_q_bf, w_o_bf,
                   g_mlp, w_up_bf, w_down_bf, g_final, seq, mem_len)
    return y.reshape(b, seq, d)
```

```python
import functools
import math

import jax
import jax.numpy as jnp
from jax.experimental import pallas as pl
from jax.experimental.pallas import tpu as pltpu

D_MODEL = 1024
ATT_WIDTH = 512
HEAD_DIM = 64
N_ATT_HEADS = 8
CONV_WIDTH = 512
CONV_K = 3
DILATED_PATTERNS = ((128, 1), (512, 4), (2048, 16))
BAND_STEPS = 128
N_MEM_HEADS = 4
MEM_HEAD_DIM = 256
D_FF = 4096
NORM_EPS = 1e-6
NEG_INF = -1e30
LOG2_E = math.log2(math.e)

LANES = 128
F32_SUBLANES = 8
BF16_SUBLANES = 16
PAIR_LANES = 2 * HEAD_DIM
assert PAIR_LANES == LANES
N_SLABS = ATT_WIDTH // LANES
STAT_LANES = LANES // N_ATT_HEADS
VMEM_LIMIT_BYTES = 58 * 1024 * 1024

TOKEN_TILE = 1024
POST_TILE = 512
FF_CHUNK = 1024
MLP_CHUNKS_PER_STAGE = (2, 1, 1)
ATTN_SUBTILES = 16
RELAYOUT_RADIX = 4


def _rms(x, g):
    ms = jnp.mean(x * x, axis=-1, keepdims=True)
    return x * jax.lax.rsqrt(ms + NORM_EPS) * g


def _dot(a, b):
    return jnp.dot(a, b, preferred_element_type=jnp.float32)


def _dot_nt(a, b):
    return jax.lax.dot_general(a, b, (((1,), (1,)), ((), ())),
                               preferred_element_type=jnp.float32)


def _params(n_axes):
    return pltpu.CompilerParams(
        dimension_semantics=("arbitrary",) * n_axes,
        vmem_limit_bytes=VMEM_LIMIT_BYTES)


def _resident(shape):
    return pl.BlockSpec(shape, lambda *_: (0,) * len(shape), pipeline_mode=pl.Buffered(1))


def _slab(j):
    return slice(j * LANES, (j + 1) * LANES)


def _split_streams(nat_ref, out4_ref, out16_ref, by4_ref):
    tm = nat_ref.shape[1]
    n4, n16 = tm // 4, tm // 16
    for r in range(RELAYOUT_RADIX):
        for j in range(N_SLABS):
            piece = nat_ref[j, pl.ds(r, n4, stride=RELAYOUT_RADIX), :]
            by4_ref[j, r * n4:(r + 1) * n4, :] = piece
            out4_ref[r, :, _slab(j)] = piece.astype(jnp.bfloat16)
    for r4 in range(RELAYOUT_RADIX):
        for a in range(RELAYOUT_RADIX):
            for j in range(N_SLABS):
                piece = by4_ref[j, pl.ds(r4 * n4 + a, n16, stride=RELAYOUT_RADIX), :]
                out16_ref[RELAYOUT_RADIX * a + r4, :, _slab(j)] = piece.astype(jnp.bfloat16)


def _merge_streams4(in4_ref, nat_ref):
    n4 = in4_ref.shape[1]
    for r in range(RELAYOUT_RADIX):
        for j in range(nat_ref.shape[0]):
            nat_ref[j, pl.ds(r, n4, stride=RELAYOUT_RADIX), :] = in4_ref[r, :, _slab(j)]


def _merge_streams16(in16_ref, by4_ref, nat_ref):
    n16 = in16_ref.shape[1]
    n4 = RELAYOUT_RADIX * n16
    n_slabs = nat_ref.shape[0]
    for r4 in range(RELAYOUT_RADIX):
        for a in range(RELAYOUT_RADIX):
            for j in range(n_slabs):
                by4_ref[j, pl.ds(r4 * n4 + a, n16, stride=RELAYOUT_RADIX), :] = (
                    in16_ref[RELAYOUT_RADIX * a + r4, :, _slab(j)])
    for r4 in range(RELAYOUT_RADIX):
        for j in range(n_slabs):
            nat_ref[j, pl.ds(r4, n4, stride=RELAYOUT_RADIX), :] = (
                by4_ref[j, r4 * n4:(r4 + 1) * n4, :])


def _in_proj_kernel(tiles_per_seq, n_cast, x_ref, g_ref, w_ref, cw_ref, gc_ref, *refs):
    cast_in, refs = refs[:n_cast], refs[n_cast:]
    (q1_ref, k1_ref, v1_ref, q4_ref, k4_ref, v4_ref, q16_ref, k16_ref, v16_ref,
     c_ref) = refs[:10]
    cast_out, (ubuf_ref, nat_ref, by4_ref) = refs[10:10 + n_cast], refs[10 + n_cast:]
    tm = x_ref.shape[0]
    for src, dst in zip(cast_in, cast_out):
        dst[...] = src[...].astype(jnp.bfloat16)

    @pl.when(pl.program_id(0) == 0)
    def _():
        ubuf_ref[tm:tm + F32_SUBLANES, :] = jnp.zeros((F32_SUBLANES, CONV_WIDTH), jnp.float32)

    h = _rms(x_ref[...], g_ref[...]).astype(jnp.bfloat16)

    def proj(col):
        return _dot(h, w_ref[:, col * ATT_WIDTH:(col + 1) * ATT_WIDTH])

    def project_streams(col, out1_ref, out4_ref, out16_ref, scale=None):
        val = proj(col) if scale is None else proj(col) * scale
        out1_ref[...] = val.astype(jnp.bfloat16)
        for j in range(N_SLABS):
            nat_ref[j] = val[:, _slab(j)]
        _split_streams(nat_ref, out4_ref, out16_ref, by4_ref)

    project_streams(0, q1_ref, q4_ref, q16_ref, LOG2_E / math.sqrt(HEAD_DIM))
    project_streams(1, k1_ref, k4_ref, k16_ref)
    project_streams(2, v1_ref, v4_ref, v16_ref)

    seq_start = pl.program_id(0) % tiles_per_seq == 0
    ubuf_ref[0:F32_SUBLANES, :] = jnp.where(seq_start, 0.0, ubuf_ref[tm:tm + F32_SUBLANES, :])
    u = proj(4) * proj(5)
    ubuf_ref[F32_SUBLANES:tm + F32_SUBLANES, :] = u
    conv = (u * cw_ref[2:3, :]
            + ubuf_ref[F32_SUBLANES - 1:tm + F32_SUBLANES - 1, :] * cw_ref[1:2, :]
            + ubuf_ref[F32_SUBLANES - 2:tm + F32_SUBLANES - 2, :] * cw_ref[0:1, :])
    y = proj(3) * conv
    c_ref[...] = _rms(y, gc_ref[...]).astype(jnp.bfloat16)


def _stream_spec(dilation, tm, tiles_per_seq):
    return pl.BlockSpec((None, dilation, tm // dilation, ATT_WIDTH),
                        lambda i: (i // tiles_per_seq, 0, i % tiles_per_seq, 0))


def _in_proj(x2d, g_mix, w_in, conv_w, g_conv_out, later_weights, batch, seq_len):
    t = x2d.shape[0]
    tm = TOKEN_TILE
    assert seq_len % tm == 0 and t == batch * seq_len
    tiles_per_seq = seq_len // tm
    n_steps = t // tm
    nat = jax.ShapeDtypeStruct((t, ATT_WIDTH), jnp.bfloat16)
    by = lambda d: jax.ShapeDtypeStruct((batch, d, seq_len // d, ATT_WIDTH), jnp.bfloat16)
    row_spec = pl.BlockSpec((tm, ATT_WIDTH), lambda i: (i, 0))
    s4, s16 = _stream_spec(4, tm, tiles_per_seq), _stream_spec(16, tm, tiles_per_seq)
    assert all(w.shape[0] % (n_steps * BF16_SUBLANES) == 0 for w in later_weights)
    cast_specs = [pl.BlockSpec((w.shape[0] // n_steps, w.shape[1]), lambda i: (i, 0))
                  for w in later_weights]
    return pl.pallas_call(
        functools.partial(_in_proj_kernel, tiles_per_seq, len(later_weights)),
        grid=(n_steps,),
        in_specs=[
            pl.BlockSpec((tm, D_MODEL), lambda i: (i, 0)),
            _resident((1, D_MODEL)),
            _resident(w_in.shape),
            _resident(conv_w.shape),
            _resident((1, CONV_WIDTH)),
        ] + cast_specs,
        out_specs=[row_spec] * 3 + [s4] * 3 + [s16] * 3 + [row_spec] + cast_specs,
        out_shape=([nat] * 3 + [by(4)] * 3 + [by(16)] * 3 + [nat]
                   + [jax.ShapeDtypeStruct(w.shape, jnp.bfloat16) for w in later_weights]),
        scratch_shapes=[pltpu.VMEM((tm + F32_SUBLANES, CONV_WIDTH), jnp.float32),
                        pltpu.VMEM((N_SLABS, tm, LANES), jnp.float32),
                        pltpu.VMEM((N_SLABS, tm, LANES), jnp.float32)],
        compiler_params=_params(1),
        name="in_proj",
    )(x2d, g_mix.reshape(1, -1), w_in, conv_w, g_conv_out.reshape(1, -1), *later_weights)


def _attn_kernel(q_ref, kc_ref, kp_ref, vc_ref, vp_ref, o_ref, stat_ref):
    s = BAND_STEPS
    streams, row_blocks = q_ref.shape[0], q_ref.shape[1] // s
    has_prev = pl.program_id(2) > 0
    qi = jax.lax.broadcasted_iota(jnp.int32, (2 * s, 2 * s), 0) & (s - 1)
    kj = jax.lax.broadcasted_iota(jnp.int32, (2 * s, 2 * s), 1)
    band = (kj >= qi) & (kj <= qi + s)
    cap = jnp.where(band, jnp.inf, NEG_INF)
    cap_first = jnp.where(band & ((kj >= s) | has_prev), jnp.inf, NEG_INF)
    first_head = jax.lax.broadcasted_iota(jnp.int32, (s, PAIR_LANES), 1) < HEAD_DIM
    ones = jnp.ones((2 * s, PAIR_LANES), jnp.bfloat16)

    def rows_of(i):
        return slice(i * s, (i + 1) * s)

    def scores(r, i, pair):
        lanes = _slab(pair)
        q2 = q_ref[r, rows_of(i), lanes]
        zero = jnp.zeros_like(q2)
        q = jnp.concatenate([jnp.where(first_head, q2, zero),
                             jnp.where(first_head, zero, q2)], axis=0)
        k_prev = kp_ref[r, :, lanes] if i == 0 else kc_ref[r, rows_of(i - 1), lanes]
        k = jnp.concatenate([k_prev, kc_ref[r, rows_of(i), lanes]], axis=0)
        return jnp.minimum(_dot_nt(q, k), cap_first if i == 0 else cap)

    def finish(r, i, pair, sc):
        lanes = _slab(pair)
        v_prev = vp_ref[r, :, lanes] if i == 0 else vc_ref[r, rows_of(i - 1), lanes]
        v = jnp.concatenate(
            [jnp.concatenate([v_prev, vc_ref[r, rows_of(i), lanes]], axis=0), ones], axis=1)
        m = jnp.max(sc, axis=-1, keepdims=True)
        p = jnp.exp2(sc - m)
        pv = _dot(p.astype(jnp.bfloat16), v)
        o_ref[r, rows_of(i), lanes] = jnp.where(first_head, pv[:s, :PAIR_LANES],
                                                pv[s:, :PAIR_LANES])
        m_all = jnp.broadcast_to(m, (2 * s, PAIR_LANES))
        l_all = pv[:, PAIR_LANES:]
        for half in range(2):
            lo = (2 * pair + half) * STAT_LANES
            mid, hi = lo + STAT_LANES // 2, lo + STAT_LANES
            stat_ref[r, rows_of(i), lo:mid] = m_all[half * s:(half + 1) * s, lo:mid]
            stat_ref[r, rows_of(i), mid:hi] = l_all[half * s:(half + 1) * s, mid:hi]

    items = [(r, i, pair) for r in range(streams) for i in range(row_blocks)
             for pair in range(N_ATT_HEADS // 2)]
    sc = scores(*items[0])
    for n, item in enumerate(items):
        sc_next = scores(*items[n + 1]) if n + 1 < len(items) else None
        finish(*item, sc)
        sc = sc_next


def _dilated_attention(q, k, v):
    b, d, steps, _ = q.shape
    row_blocks = min(ATTN_SUBTILES, steps // BAND_STEPS)
    streams = ATTN_SUBTILES // row_blocks
    tile_rows = row_blocks * BAND_STEPS
    assert steps % tile_rows == 0 and d % streams == 0
    cur = pl.BlockSpec((None, streams, tile_rows, ATT_WIDTH), lambda bi, g, n: (bi, g, n, 0))
    prev = pl.BlockSpec((None, streams, BAND_STEPS, ATT_WIDTH),
                        lambda bi, g, n: (bi, g, jnp.maximum(n * row_blocks - 1, 0), 0))
    cur_lse = pl.BlockSpec((None, streams, tile_rows, LANES), lambda bi, g, n: (bi, g, n, 0))
    return pl.pallas_call(
        _attn_kernel,
        grid=(b, d // streams, steps // tile_rows),
        in_specs=[cur, cur, prev, cur, prev],
        out_specs=[cur, cur_lse],
        out_shape=[jax.ShapeDtypeStruct(q.shape, jnp.float32),
                   jax.ShapeDtypeStruct((b, d, steps, LANES), jnp.float32)],
        compiler_params=_params(3),
        name=f"dilated_attn_d{d}",
    )(q, k, k, v, v)


def _post_attn_kernel(o1_ref, l1_ref, o4_ref, l4_ref, o16_ref, l16_ref, c_ref, x_ref,
                      ga_ref, wout_ref, gx_ref, wq_ref, kv_ref, wo_ref,
                      gm_ref, wu_ref, wd_ref, gf_ref, y_ref,
                      on4_ref, ln4_ref, on16_ref, ln16_ref, by4_ref, xmid_ref):
    i = pl.program_id(0)

    @pl.when(i == 0)
    def _():
        xmid_ref[...] = jnp.zeros(xmid_ref.shape, jnp.float32)

    assert sum(MLP_CHUNKS_PER_STAGE) * FF_CHUNK == D_FF
    y = xmid_ref[(i + 1) % 2]
    h_mlp = _rms(y, gm_ref[...]).astype(jnp.bfloat16)
    chunk_ids = iter(range(D_FF // FF_CHUNK))

    def mlp_stage(y, stage):
        for _ in range(MLP_CHUNKS_PER_STAGE[stage]):
            c = next(chunk_ids)
            cols = slice(c * FF_CHUNK, (c + 1) * FF_CHUNK)
            a = jnp.maximum(_dot(h_mlp, wu_ref[:, cols]), 0.0)
            y = y + _dot((a * a).astype(jnp.bfloat16), wd_ref[cols, :])
        return y

    _merge_streams4(o4_ref, on4_ref)
    _merge_streams4(l4_ref, ln4_ref)
    _merge_streams16(o16_ref, by4_ref, on16_ref)
    _merge_streams16(l16_ref, by4_ref, ln16_ref)
    stats = [l1_ref[...], ln4_ref[0], ln16_ref[0]]
    lane = jax.lax.broadcasted_iota(jnp.int32, stats[0].shape, 1)
    top = jnp.maximum(jnp.maximum(stats[0], stats[1]), stats[2])
    scale = [jnp.exp2(st - top) for st in stats]
    sums = [jnp.take_along_axis(st, lane ^ (STAT_LANES // 2), axis=1) for st in stats]
    inv = 1.0 / (scale[0] * sums[0] + scale[1] * sums[1] + scale[2] * sums[2])
    weights = [sc * inv for sc in scale]
    slabs = []
    for j in range(N_SLABS):
        src = (2 * j + lane // HEAD_DIM) * STAT_LANES
        w = [jnp.take_along_axis(wp, src, axis=1) for wp in weights]
        slabs.append(w[0] * o1_ref[:, _slab(j)] + w[1] * on4_ref[j] + w[2] * on16_ref[j])
    attn = jnp.concatenate(slabs, axis=-1)
    attn_n = _rms(attn, ga_ref[...]).astype(jnp.bfloat16)
    y = mlp_stage(y, 0)
    x = (x_ref[...] + _dot(attn_n, wout_ref[0:ATT_WIDTH, :])
         + _dot(c_ref[...], wout_ref[ATT_WIDTH:, :]))

    y = mlp_stage(y, 1)

    h = _rms(x, gx_ref[...]).astype(jnp.bfloat16)
    q = (_dot(h, wq_ref[...]) * (LOG2_E / math.sqrt(MEM_HEAD_DIM))).astype(jnp.bfloat16)
    head = lambda hd: slice(hd * MEM_HEAD_DIM, (hd + 1) * MEM_HEAD_DIM)
    scores = [_dot_nt(q[:, head(hd)], kv_ref[:, head(hd)]) for hd in range(N_MEM_HEADS)]

    y = mlp_stage(y, 2)
    y_ref[...] = _rms(y, gf_ref[...])

    heads = []
    for hd in range(N_MEM_HEADS):
        sc = scores[hd]
        m = jnp.max(sc, axis=-1, keepdims=True)
        p = jnp.exp2(sc - m)
        l = jnp.sum(p, axis=-1, keepdims=True)
        v = kv_ref[:, D_MODEL + hd * MEM_HEAD_DIM:D_MODEL + (hd + 1) * MEM_HEAD_DIM]
        heads.append((_dot(p.astype(jnp.bfloat16), v) / l).astype(jnp.bfloat16))
    xmid_ref[i % 2] = x + _dot(jnp.concatenate(heads, axis=-1), wo_ref[...])


def _post_attn(o1, l1, o4, l4, o16, l16, convn, x2d, kv, g_attn_out, w_out, g_xattn, w_q, w_o,
               g_mlp, w_up, w_down, g_final, seq_len, mem_len):
    t = x2d.shape[0]
    tm = POST_TILE
    assert seq_len % tm == 0
    tiles_per_seq = seq_len // tm
    n_tiles = t // tm
    tile_in = lambda i: jnp.minimum(i, n_tiles - 1)
    row_spec = lambda width: pl.BlockSpec((tm, width), lambda i: (tile_in(i), 0))
    stream_spec = lambda d, width: pl.BlockSpec(
        (None, d, tm // d, width),
        lambda i: (tile_in(i) // tiles_per_seq, 0, tile_in(i) % tiles_per_seq, 0))
    x_spec = pl.BlockSpec((tm, D_MODEL), lambda i: (tile_in(i), 0))
    gain = lambda g: g.reshape(1, -1)
    nat = pltpu.VMEM((N_SLABS, tm, LANES), jnp.float32)
    nat_lse = pltpu.VMEM((1, tm, LANES), jnp.float32)
    return pl.pallas_call(
        _post_attn_kernel,
        grid=(n_tiles + 1,),
        in_specs=[row_spec(ATT_WIDTH), row_spec(LANES),
                  stream_spec(4, ATT_WIDTH), stream_spec(4, LANES),
                  stream_spec(16, ATT_WIDTH), stream_spec(16, LANES),
                  row_spec(ATT_WIDTH), x_spec,
                  _resident((1, ATT_WIDTH)), _resident(w_out.shape),
                  _resident((1, D_MODEL)), _resident(w_q.shape),
                  pl.BlockSpec((mem_len, 2 * D_MODEL),
                               lambda i: (tile_in(i) // tiles_per_seq, 0)),
                  _resident(w_o.shape),
                  _resident((1, D_MODEL)), _resident(w_up.shape), _resident(w_down.shape),
                  _resident((1, D_MODEL))],
        out_specs=pl.BlockSpec((tm, D_MODEL), lambda i: (jnp.maximum(i - 1, 0), 0)),
        out_shape=jax.ShapeDtypeStruct((t, D_MODEL), jnp.float32),
        scratch_shapes=[nat, nat_lse, nat, nat_lse, nat,
                        pltpu.VMEM((2, tm, D_MODEL), jnp.float32)],
        compiler_params=_params(1),
        name="post_attn",
    )(o1, l1, o4, l4, o16, l16, convn, x2d, gain(g_attn_out), w_out, gain(g_xattn), w_q, kv, w_o,
      gain(g_mlp), w_up, w_down, gain(g_final))


def _mem_kv_kernel(m_ref, g_ref, w_ref, kv_ref):
    h = _rms(m_ref[...], g_ref[...]).astype(jnp.bfloat16)
    kv_ref[...] = _dot(h, w_ref[...]).astype(jnp.bfloat16)


def _mem_kv(mem2d, g_mem, w_kv):
    rows = mem2d.shape[0]
    tm = TOKEN_TILE
    return pl.pallas_call(
        _mem_kv_kernel,
        grid=(rows // tm,),
        in_specs=[pl.BlockSpec((tm, D_MODEL), lambda i: (i, 0)),
                  _resident((1, D_MODEL)), _resident(w_kv.shape)],
        out_specs=pl.BlockSpec((tm, 2 * D_MODEL), lambda i: (i, 0)),
        out_shape=jax.ShapeDtypeStruct((rows, 2 * D_MODEL), jnp.bfloat16),
        compiler_params=_params(1),
        name="mem_kv_proj",
    )(mem2d, g_mem.reshape(1, -1), w_kv)


def kernel(x, mem, g_mix, w_in, conv_w, g_attn_out, g_conv_out, w_out, g_xattn, g_mem,
           w_q_mem, w_kv_mem, w_o_mem, g_mlp, w_up, w_down, g_final):
    b, seq, d = x.shape
    mem_len = mem.shape[1]
    assert tuple(dil for _, dil in DILATED_PATTERNS) == (1, 4, 16)
    assert all(win // dil == BAND_STEPS for win, dil in DILATED_PATTERNS)
    bf = lambda w: w.astype(jnp.bfloat16)
    x2d = x.reshape(b * seq, d)

    (q1, k1, v1, q4, k4, v4, q16, k16, v16, convn,
     w_out_bf, w_q_bf, w_o_bf, w_up_bf, w_down_bf) = _in_proj(
        x2d, g_mix, bf(w_in), conv_w, g_conv_out,
        (w_out, w_q_mem, w_o_mem, w_up, w_down), b, seq)
    as_stream = lambda a: a.reshape(b, 1, seq, ATT_WIDTH)
    o1, l1 = _dilated_attention(as_stream(q1), as_stream(k1), as_stream(v1))
    o4, l4 = _dilated_attention(q4, k4, v4)
    o16, l16 = _dilated_attention(q16, k16, v16)
    flat = lambda a: a.reshape(b * seq, a.shape[-1])
    kv = _mem_kv(mem.reshape(b * mem_len, d), g_mem, bf(w_kv_mem))
    y = _post_attn(flat(o1), flat(l1), o4, l4, o16, l16, convn, x2d, kv,
                   g_attn_out, w_out_bf, g_xattn, w_q_bf, w_o_bf,
                   g_mlp, w_up_bf, w_down_bf, g_final, seq, mem_len)
    return y.reshape(b, seq, d)
```

```python
import functools
import math

import jax
import jax.numpy as jnp
from jax.experimental import pallas as pl
from jax.experimental.pallas import tpu as pltpu

D_MODEL = 1024
ATT_WIDTH = 512
HEAD_DIM = 64
N_ATT_HEADS = 8
CONV_WIDTH = 512
CONV_K = 3
DILATED_PATTERNS = ((128, 1), (512, 4), (2048, 16))
BAND_STEPS = 128
N_MEM_HEADS = 4
MEM_HEAD_DIM = 256
D_FF = 4096
NORM_EPS = 1e-6
NEG_INF = -1e30
LOG2_E = math.log2(math.e)

LANES = 128
F32_SUBLANES = 8
BF16_SUBLANES = 16
PAIR_LANES = 2 * HEAD_DIM
assert PAIR_LANES == LANES
N_SLABS = ATT_WIDTH // LANES
STAT_LANES = LANES // N_ATT_HEADS
VMEM_LIMIT_BYTES = 58 * 1024 * 1024

TOKEN_TILE = 1024
POST_TILE = 512
FF_CHUNK = 1024
MLP_CHUNKS_PER_STAGE = (2, 1, 1)
ATTN_SUBTILES = 32
RELAYOUT_RADIX = 4


def _rms(x, g):
    ms = jnp.mean(x * x, axis=-1, keepdims=True)
    return x * jax.lax.rsqrt(ms + NORM_EPS) * g


def _dot(a, b):
    return jnp.dot(a, b, preferred_element_type=jnp.float32)


def _dot_nt(a, b):
    return jax.lax.dot_general(a, b, (((1,), (1,)), ((), ())),
                               preferred_element_type=jnp.float32)


def _params(n_axes):
    return pltpu.CompilerParams(
        dimension_semantics=("arbitrary",) * n_axes,
        vmem_limit_bytes=VMEM_LIMIT_BYTES)


def _resident(shape):
    return pl.BlockSpec(shape, lambda *_: (0,) * len(shape), pipeline_mode=pl.Buffered(1))


def _slab(j):
    return slice(j * LANES, (j + 1) * LANES)


def _split_streams(nat_ref, out4_ref, out16_ref, by4_ref):
    tm = nat_ref.shape[1]
    n4, n16 = tm // 4, tm // 16
    for r in range(RELAYOUT_RADIX):
        for j in range(N_SLABS):
            piece = nat_ref[j, pl.ds(r, n4, stride=RELAYOUT_RADIX), :]
            by4_ref[j, r * n4:(r + 1) * n4, :] = piece
            out4_ref[r, :, _slab(j)] = piece.astype(jnp.bfloat16)
    for r4 in range(RELAYOUT_RADIX):
        for a in range(RELAYOUT_RADIX):
            for j in range(N_SLABS):
                piece = by4_ref[j, pl.ds(r4 * n4 + a, n16, stride=RELAYOUT_RADIX), :]
                out16_ref[RELAYOUT_RADIX * a + r4, :, _slab(j)] = piece.astype(jnp.bfloat16)


def _merge_streams4(in4_ref, nat_ref):
    n4 = in4_ref.shape[1]
    for r in range(RELAYOUT_RADIX):
        for j in range(nat_ref.shape[0]):
            nat_ref[j, pl.ds(r, n4, stride=RELAYOUT_RADIX), :] = in4_ref[r, :, _slab(j)]


def _merge_streams16(in16_ref, by4_ref, nat_ref):
    n16 = in16_ref.shape[1]
    n4 = RELAYOUT_RADIX * n16
    n_slabs = nat_ref.shape[0]
    for r4 in range(RELAYOUT_RADIX):
        for a in range(RELAYOUT_RADIX):
            for j in range(n_slabs):
                by4_ref[j, pl.ds(r4 * n4 + a, n16, stride=RELAYOUT_RADIX), :] = (
                    in16_ref[RELAYOUT_RADIX * a + r4, :, _slab(j)])
    for r4 in range(RELAYOUT_RADIX):
        for j in range(n_slabs):
            nat_ref[j, pl.ds(r4, n4, stride=RELAYOUT_RADIX), :] = (
                by4_ref[j, r4 * n4:(r4 + 1) * n4, :])


def _in_proj_kernel(tiles_per_seq, n_cast, x_ref, g_ref, w_ref, cw_ref, gc_ref, *refs):
    cast_in, refs = refs[:n_cast], refs[n_cast:]
    (q1_ref, k1_ref, v1_ref, q4_ref, k4_ref, v4_ref, q16_ref, k16_ref, v16_ref,
     c_ref) = refs[:10]
    cast_out, (ubuf_ref, nat_ref, by4_ref) = refs[10:10 + n_cast], refs[10 + n_cast:]
    tm = x_ref.shape[0]
    for src, dst in zip(cast_in, cast_out):
        dst[...] = src[...].astype(jnp.bfloat16)

    @pl.when(pl.program_id(0) == 0)
    def _():
        ubuf_ref[tm:tm + F32_SUBLANES, :] = jnp.zeros((F32_SUBLANES, CONV_WIDTH), jnp.float32)

    h = _rms(x_ref[...], g_ref[...]).astype(jnp.bfloat16)

    def proj(col):
        return _dot(h, w_ref[:, col * ATT_WIDTH:(col + 1) * ATT_WIDTH])

    def project_streams(col, out1_ref, out4_ref, out16_ref, scale=None):
        val = proj(col) if scale is None else proj(col) * scale
        out1_ref[...] = val.astype(jnp.bfloat16)
        for j in range(N_SLABS):
            nat_ref[j] = val[:, _slab(j)]
        _split_streams(nat_ref, out4_ref, out16_ref, by4_ref)

    project_streams(0, q1_ref, q4_ref, q16_ref, LOG2_E / math.sqrt(HEAD_DIM))
    project_streams(1, k1_ref, k4_ref, k16_ref)
    project_streams(2, v1_ref, v4_ref, v16_ref)

    seq_start = pl.program_id(0) % tiles_per_seq == 0
    ubuf_ref[0:F32_SUBLANES, :] = jnp.where(seq_start, 0.0, ubuf_ref[tm:tm + F32_SUBLANES, :])
    u = proj(4) * proj(5)
    ubuf_ref[F32_SUBLANES:tm + F32_SUBLANES, :] = u
    conv = (u * cw_ref[2:3, :]
            + ubuf_ref[F32_SUBLANES - 1:tm + F32_SUBLANES - 1, :] * cw_ref[1:2, :]
            + ubuf_ref[F32_SUBLANES - 2:tm + F32_SUBLANES - 2, :] * cw_ref[0:1, :])
    y = proj(3) * conv
    c_ref[...] = _rms(y, gc_ref[...]).astype(jnp.bfloat16)


def _stream_spec(dilation, tm, tiles_per_seq):
    return pl.BlockSpec((None, dilation, tm // dilation, ATT_WIDTH),
                        lambda i: (i // tiles_per_seq, 0, i % tiles_per_seq, 0))


def _in_proj(x2d, g_mix, w_in, conv_w, g_conv_out, later_weights, batch, seq_len):
    t = x2d.shape[0]
    tm = TOKEN_TILE
    assert seq_len % tm == 0 and t == batch * seq_len
    tiles_per_seq = seq_len // tm
    n_steps = t // tm
    nat = jax.ShapeDtypeStruct((t, ATT_WIDTH), jnp.bfloat16)
    by = lambda d: jax.ShapeDtypeStruct((batch, d, seq_len // d, ATT_WIDTH), jnp.bfloat16)
    row_spec = pl.BlockSpec((tm, ATT_WIDTH), lambda i: (i, 0))
    s4, s16 = _stream_spec(4, tm, tiles_per_seq), _stream_spec(16, tm, tiles_per_seq)
    assert all(w.shape[0] % (n_steps * BF16_SUBLANES) == 0 for w in later_weights)
    cast_specs = [pl.BlockSpec((w.shape[0] // n_steps, w.shape[1]), lambda i: (i, 0))
                  for w in later_weights]
    return pl.pallas_call(
        functools.partial(_in_proj_kernel, tiles_per_seq, len(later_weights)),
        grid=(n_steps,),
        in_specs=[
            pl.BlockSpec((tm, D_MODEL), lambda i: (i, 0)),
            _resident((1, D_MODEL)),
            _resident(w_in.shape),
            _resident(conv_w.shape),
            _resident((1, CONV_WIDTH)),
        ] + cast_specs,
        out_specs=[row_spec] * 3 + [s4] * 3 + [s16] * 3 + [row_spec] + cast_specs,
        out_shape=([nat] * 3 + [by(4)] * 3 + [by(16)] * 3 + [nat]
                   + [jax.ShapeDtypeStruct(w.shape, jnp.bfloat16) for w in later_weights]),
        scratch_shapes=[pltpu.VMEM((tm + F32_SUBLANES, CONV_WIDTH), jnp.float32),
                        pltpu.VMEM((N_SLABS, tm, LANES), jnp.float32),
                        pltpu.VMEM((N_SLABS, tm, LANES), jnp.float32)],
        compiler_params=_params(1),
        name="in_proj",
    )(x2d, g_mix.reshape(1, -1), w_in, conv_w, g_conv_out.reshape(1, -1), *later_weights)


def _attn_kernel(q_ref, kc_ref, kp_ref, vc_ref, vp_ref, o_ref, stat_ref):
    s = BAND_STEPS
    streams, row_blocks = q_ref.shape[0], q_ref.shape[1] // s
    has_prev = pl.program_id(2) > 0
    qi = jax.lax.broadcasted_iota(jnp.int32, (2 * s, 2 * s), 0) & (s - 1)
    kj = jax.lax.broadcasted_iota(jnp.int32, (2 * s, 2 * s), 1)
    band = (kj >= qi) & (kj <= qi + s)
    cap = jnp.where(band, jnp.inf, NEG_INF)
    cap_first = jnp.where(band & ((kj >= s) | has_prev), jnp.inf, NEG_INF)
    first_head = jax.lax.broadcasted_iota(jnp.int32, (s, PAIR_LANES), 1) < HEAD_DIM
    ones = jnp.ones((2 * s, PAIR_LANES), jnp.bfloat16)

    def rows_of(i):
        return slice(i * s, (i + 1) * s)

    def scores(r, i, pair):
        lanes = _slab(pair)
        q2 = q_ref[r, rows_of(i), lanes]
        zero = jnp.zeros_like(q2)
        q = jnp.concatenate([jnp.where(first_head, q2, zero),
                             jnp.where(first_head, zero, q2)], axis=0)
        k_prev = kp_ref[r, :, lanes] if i == 0 else kc_ref[r, rows_of(i - 1), lanes]
        k = jnp.concatenate([k_prev, kc_ref[r, rows_of(i), lanes]], axis=0)
        return jnp.minimum(_dot_nt(q, k), cap_first if i == 0 else cap)

    def finish(r, i, pair, sc):
        lanes = _slab(pair)
        v_prev = vp_ref[r, :, lanes] if i == 0 else vc_ref[r, rows_of(i - 1), lanes]
        v = jnp.concatenate(
            [jnp.concatenate([v_prev, vc_ref[r, rows_of(i), lanes]], axis=0), ones], axis=1)
        m = jnp.max(sc, axis=-1, keepdims=True)
        p = jnp.exp2(sc - m)
        pv = _dot(p.astype(jnp.bfloat16), v)
        o_ref[r, rows_of(i), lanes] = jnp.where(first_head, pv[:s, :PAIR_LANES],
                                                pv[s:, :PAIR_LANES])
        m_all = jnp.broadcast_to(m, (2 * s, PAIR_LANES))
        l_all = pv[:, PAIR_LANES:]
        for half in range(2):
            lo = (2 * pair + half) * STAT_LANES
            mid, hi = lo + STAT_LANES // 2, lo + STAT_LANES
            stat_ref[r, rows_of(i), lo:mid] = m_all[half * s:(half + 1) * s, lo:mid]
            stat_ref[r, rows_of(i), mid:hi] = l_all[half * s:(half + 1) * s, mid:hi]

    items = [(r, i, pair) for r in range(streams) for i in range(row_blocks)
             for pair in range(N_ATT_HEADS // 2)]
    sc = scores(*items[0])
    for n, item in enumerate(items):
        sc_next = scores(*items[n + 1]) if n + 1 < len(items) else None
        finish(*item, sc)
        sc = sc_next


def _dilated_attention(q, k, v):
    b, d, steps, _ = q.shape
    row_blocks = min(ATTN_SUBTILES, steps // BAND_STEPS)
    streams = ATTN_SUBTILES // row_blocks
    tile_rows = row_blocks * BAND_STEPS
    assert steps % tile_rows == 0 and d % streams == 0
    cur = pl.BlockSpec((None, streams, tile_rows, ATT_WIDTH), lambda bi, g, n: (bi, g, n, 0))
    prev = pl.BlockSpec((None, streams, BAND_STEPS, ATT_WIDTH),
                        lambda bi, g, n: (bi, g, jnp.maximum(n * row_blocks - 1, 0), 0))
    cur_lse = pl.BlockSpec((None, streams, tile_rows, LANES), lambda bi, g, n: (bi, g, n, 0))
    return pl.pallas_call(
        _attn_kernel,
        grid=(b, d // streams, steps // tile_rows),
        in_specs=[cur, cur, prev, cur, prev],
        out_specs=[cur, cur_lse],
        out_shape=[jax.ShapeDtypeStruct(q.shape, jnp.float32),
                   jax.ShapeDtypeStruct((b, d, steps, LANES), jnp.float32)],
        compiler_params=_params(3),
        name=f"dilated_attn_d{d}",
    )(q, k, k, v, v)


def _post_attn_kernel(o1_ref, l1_ref, o4_ref, l4_ref, o16_ref, l16_ref, c_ref, x_ref,
                      ga_ref, wout_ref, gx_ref, wq_ref, kv_ref, wo_ref,
                      gm_ref, wu_ref, wd_ref, gf_ref, y_ref,
                      on4_ref, ln4_ref, on16_ref, ln16_ref, by4_ref, xmid_ref):
    i = pl.program_id(0)

    @pl.when(i == 0)
    def _():
        xmid_ref[...] = jnp.zeros(xmid_ref.shape, jnp.float32)

    assert sum(MLP_CHUNKS_PER_STAGE) * FF_CHUNK == D_FF
    y = xmid_ref[(i + 1) % 2]
    h_mlp = _rms(y, gm_ref[...]).astype(jnp.bfloat16)
    chunk_ids = iter(range(D_FF // FF_CHUNK))

    def mlp_stage(y, stage):
        for _ in range(MLP_CHUNKS_PER_STAGE[stage]):
            c = next(chunk_ids)
            cols = slice(c * FF_CHUNK, (c + 1) * FF_CHUNK)
            a = jnp.maximum(_dot(h_mlp, wu_ref[:, cols]), 0.0)
            y = y + _dot((a * a).astype(jnp.bfloat16), wd_ref[cols, :])
        return y

    _merge_streams4(o4_ref, on4_ref)
    _merge_streams4(l4_ref, ln4_ref)
    _merge_streams16(o16_ref, by4_ref, on16_ref)
    _merge_streams16(l16_ref, by4_ref, ln16_ref)
    stats = [l1_ref[...], ln4_ref[0], ln16_ref[0]]
    lane = jax.lax.broadcasted_iota(jnp.int32, stats[0].shape, 1)
    top = jnp.maximum(jnp.maximum(stats[0], stats[1]), stats[2])
    scale = [jnp.exp2(st - top) for st in stats]
    sums = [jnp.take_along_axis(st, lane ^ (STAT_LANES // 2), axis=1) for st in stats]
    inv = 1.0 / (scale[0] * sums[0] + scale[1] * sums[1] + scale[2] * sums[2])
    weights = [sc * inv for sc in scale]
    slabs = []
    for j in range(N_SLABS):
        src = (2 * j + lane // HEAD_DIM) * STAT_LANES
        w = [jnp.take_along_axis(wp, src, axis=1) for wp in weights]
        slabs.append(w[0] * o1_ref[:, _slab(j)] + w[1] * on4_ref[j] + w[2] * on16_ref[j])
    attn = jnp.concatenate(slabs, axis=-1)
    attn_n = _rms(attn, ga_ref[...]).astype(jnp.bfloat16)
    y = mlp_stage(y, 0)
    x = (x_ref[...] + _dot(attn_n, wout_ref[0:ATT_WIDTH, :])
         + _dot(c_ref[...], wout_ref[ATT_WIDTH:, :]))

    y = mlp_stage(y, 1)

    h = _rms(x, gx_ref[...]).astype(jnp.bfloat16)
    q = (_dot(h, wq_ref[...]) * (LOG2_E / math.sqrt(MEM_HEAD_DIM))).astype(jnp.bfloat16)
    head = lambda hd: slice(hd * MEM_HEAD_DIM, (hd + 1) * MEM_HEAD_DIM)
    scores = [_dot_nt(q[:, head(hd)], kv_ref[:, head(hd)]) for hd in range(N_MEM_HEADS)]

    y = mlp_stage(y, 2)
    y_ref[...] = _rms(y, gf_ref[...])

    heads = []
    for hd in range(N_MEM_HEADS):
        sc = scores[hd]
        m = jnp.max(sc, axis=-1, keepdims=True)
        p = jnp.exp2(sc - m)
        l = jnp.sum(p, axis=-1, keepdims=True)
        v = kv_ref[:, D_MODEL + hd * MEM_HEAD_DIM:D_MODEL + (hd + 1) * MEM_HEAD_DIM]
        heads.append((_dot(p.astype(jnp.bfloat16), v) / l).astype(jnp.bfloat16))
    xmid_ref[i % 2] = x + _dot(jnp.concatenate(heads, axis=-1), wo_ref[...])


def _post_attn(o1, l1, o4, l4, o16, l16, convn, x2d, kv, g_attn_out, w_out, g_xattn, w_q, w_o,
               g_mlp, w_up, w_down, g_final, seq_len, mem_len):
    t = x2d.shape[0]
    tm = POST_TILE
    assert seq_len % tm == 0
    tiles_per_seq = seq_len // tm
    n_tiles = t // tm
    tile_in = lambda i: jnp.minimum(i, n_tiles - 1)
    row_spec = lambda width: pl.BlockSpec((tm, width), lambda i: (tile_in(i), 0))
    stream_spec = lambda d, width: pl.BlockSpec(
        (None, d, tm // d, width),
        lambda i: (tile_in(i) // tiles_per_seq, 0, tile_in(i) % tiles_per_seq, 0))
    x_spec = pl.BlockSpec((tm, D_MODEL), lambda i: (tile_in(i), 0))
    gain = lambda g: g.reshape(1, -1)
    nat = pltpu.VMEM((N_SLABS, tm, LANES), jnp.float32)
    nat_lse = pltpu.VMEM((1, tm, LANES), jnp.float32)
    return pl.pallas_call(
        _post_attn_kernel,
        grid=(n_tiles + 1,),
        in_specs=[row_spec(ATT_WIDTH), row_spec(LANES),
                  stream_spec(4, ATT_WIDTH), stream_spec(4, LANES),
                  stream_spec(16, ATT_WIDTH), stream_spec(16, LANES),
                  row_spec(ATT_WIDTH), x_spec,
                  _resident((1, ATT_WIDTH)), _resident(w_out.shape),
                  _resident((1, D_MODEL)), _resident(w_q.shape),
                  pl.BlockSpec((mem_len, 2 * D_MODEL),
                               lambda i: (tile_in(i) // tiles_per_seq, 0)),
                  _resident(w_o.shape),
                  _resident((1, D_MODEL)), _resident(w_up.shape), _resident(w_down.shape),
                  _resident((1, D_MODEL))],
        out_specs=pl.BlockSpec((tm, D_MODEL), lambda i: (jnp.maximum(i - 1, 0), 0)),
        out_shape=jax.ShapeDtypeStruct((t, D_MODEL), jnp.float32),
        scratch_shapes=[nat, nat_lse, nat, nat_lse, nat,
                        pltpu.VMEM((2, tm, D_MODEL), jnp.float32)],
        compiler_params=_params(1),
        name="post_attn",
    )(o1, l1, o4, l4, o16, l16, convn, x2d, gain(g_attn_out), w_out, gain(g_xattn), w_q, kv, w_o,
      gain(g_mlp), w_up, w_down, gain(g_final))


def _mem_kv_kernel(m_ref, g_ref, w_ref, kv_ref):
    h = _rms(m_ref[...], g_ref[...]).astype(jnp.bfloat16)
    kv_ref[...] = _dot(h, w_ref[...]).astype(jnp.bfloat16)


def _mem_kv(mem2d, g_mem, w_kv):
    rows = mem2d.shape[0]
    tm = TOKEN_TILE
    return pl.pallas_call(
        _mem_kv_kernel,
        grid=(rows // tm,),
        in_specs=[pl.BlockSpec((tm, D_MODEL), lambda i: (i, 0)),
                  _resident((1, D_MODEL)), _resident(w_kv.shape)],
        out_specs=pl.BlockSpec((tm, 2 * D_MODEL), lambda i: (i, 0)),
        out_shape=jax.ShapeDtypeStruct((rows, 2 * D_MODEL), jnp.bfloat16),
        compiler_params=_params(1),
        name="mem_kv_proj",
    )(mem2d, g_mem.reshape(1, -1), w_kv)


def kernel(x, mem, g_mix, w_in, conv_w, g_attn_out, g_conv_out, w_out, g_xattn, g_mem,
           w_q_mem, w_kv_mem, w_o_mem, g_mlp, w_up, w_down, g_final):
    b, seq, d = x.shape
    mem_len = mem.shape[1]
    assert tuple(dil for _, dil in DILATED_PATTERNS) == (1, 4, 16)
    assert all(win // dil == BAND_STEPS for win, dil in DILATED_PATTERNS)
    bf = lambda w: w.astype(jnp.bfloat16)
    x2d = x.reshape(b * seq, d)

    (q1, k1, v1, q4, k4, v4, q16, k16, v16, convn,
     w_out_bf, w_q_bf, w_o_bf, w_up_bf, w_down_bf) = _in_proj(
        x2d, g_mix, bf(w_in), conv_w, g_conv_out,
        (w_out, w_q_mem, w_o_mem, w_up, w_down), b, seq)
    as_stream = lambda a: a.reshape(b, 1, seq, ATT_WIDTH)
    o1, l1 = _dilated_attention(as_stream(q1), as_stream(k1), as_stream(v1))
    o4, l4 = _dilated_attention(q4, k4, v4)
    o16, l16 = _dilated_attention(q16, k16, v16)
    flat = lambda a: a.reshape(b * seq, a.shape[-1])
    kv = _mem_kv(mem.reshape(b * mem_len, d), g_mem, bf(w_kv_mem))
    y = _post_attn(flat(o1), flat(l1), o4, l4, o16, l16, convn, x2d, kv,
                   g_attn_out, w_out_bf, g_xattn, w_q_bf, w_o_bf,
                   g_mlp, w_up_bf, w_down_bf, g_final, seq, mem_len)
    return y.reshape(b, seq, d)
```

```python
import functools
import math

import jax
import jax.numpy as jnp
from jax.experimental import pallas as pl
from jax.experimental.pallas import tpu as pltpu

D_MODEL = 1024
ATT_WIDTH = 512
HEAD_DIM = 64
N_ATT_HEADS = 8
CONV_WIDTH = 512
CONV_K = 3
DILATED_PATTERNS = ((128, 1), (512, 4), (2048, 16))
BAND_STEPS = 128
N_MEM_HEADS = 4
MEM_HEAD_DIM = 256
D_FF = 4096
NORM_EPS = 1e-6
NEG_INF = -1e30
LOG2_E = math.log2(math.e)

LANES = 128
F32_SUBLANES = 8
BF16_SUBLANES = 16
PAIR_LANES = 2 * HEAD_DIM
assert PAIR_LANES == LANES
N_SLABS = ATT_WIDTH // LANES
STAT_LANES = LANES // N_ATT_HEADS
VMEM_LIMIT_BYTES = 58 * 1024 * 1024

TOKEN_TILE = 1024
POST_TILE = 512
FF_CHUNK = 1024
MLP_CHUNKS_PER_STAGE = (2, 1, 1)
ATTN_SUBTILES = 32
RELAYOUT_RADIX = 4


def _rms(x, g):
    ms = jnp.mean(x * x, axis=-1, keepdims=True)
    return x * jax.lax.rsqrt(ms + NORM_EPS) * g


def _dot(a, b):
    return jnp.dot(a, b, preferred_element_type=jnp.float32)


def _dot_nt(a, b):
    return jax.lax.dot_general(a, b, (((1,), (1,)), ((), ())),
                               preferred_element_type=jnp.float32)


def _params(n_axes):
    return pltpu.CompilerParams(
        dimension_semantics=("arbitrary",) * n_axes,
        vmem_limit_bytes=VMEM_LIMIT_BYTES)


def _resident(shape):
    return pl.BlockSpec(shape, lambda *_: (0,) * len(shape), pipeline_mode=pl.Buffered(1))


def _slab(j):
    return slice(j * LANES, (j + 1) * LANES)


def _split_streams(nat_ref, out4_ref, out16_ref, by4_ref):
    tm = nat_ref.shape[1]
    n4, n16 = tm // 4, tm // 16
    for r in range(RELAYOUT_RADIX):
        for j in range(N_SLABS):
            piece = nat_ref[j, pl.ds(r, n4, stride=RELAYOUT_RADIX), :]
            by4_ref[j, r * n4:(r + 1) * n4, :] = piece
            out4_ref[r, :, _slab(j)] = piece.astype(jnp.bfloat16)
    for r4 in range(RELAYOUT_RADIX):
        for a in range(RELAYOUT_RADIX):
            for j in range(N_SLABS):
                piece = by4_ref[j, pl.ds(r4 * n4 + a, n16, stride=RELAYOUT_RADIX), :]
                out16_ref[RELAYOUT_RADIX * a + r4, :, _slab(j)] = piece.astype(jnp.bfloat16)


def _merge_streams4(in4_ref, nat_ref):
    n4 = in4_ref.shape[1]
    for r in range(RELAYOUT_RADIX):
        for j in range(nat_ref.shape[0]):
            nat_ref[j, pl.ds(r, n4, stride=RELAYOUT_RADIX), :] = in4_ref[r, :, _slab(j)]


def _merge_streams16(in16_ref, by4_ref, nat_ref):
    n16 = in16_ref.shape[1]
    n4 = RELAYOUT_RADIX * n16
    n_slabs = nat_ref.shape[0]
    for r4 in range(RELAYOUT_RADIX):
        for a in range(RELAYOUT_RADIX):
            for j in range(n_slabs):
                by4_ref[j, pl.ds(r4 * n4 + a, n16, stride=RELAYOUT_RADIX), :] = (
                    in16_ref[RELAYOUT_RADIX * a + r4, :, _slab(j)])
    for r4 in range(RELAYOUT_RADIX):
        for j in range(n_slabs):
            nat_ref[j, pl.ds(r4, n4, stride=RELAYOUT_RADIX), :] = (
                by4_ref[j, r4 * n4:(r4 + 1) * n4, :])


def _in_proj_kernel(tiles_per_seq, n_cast, x_ref, g_ref, w_ref, cw_ref, gc_ref, *refs):
    cast_in, refs = refs[:n_cast], refs[n_cast:]
    (q1_ref, k1_ref, v1_ref, q4_ref, k4_ref, v4_ref, q16_ref, k16_ref, v16_ref,
     c_ref) = refs[:10]
    cast_out, (ubuf_ref, nat_ref, by4_ref) = refs[10:10 + n_cast], refs[10 + n_cast:]
    tm = x_ref.shape[0]
    for src, dst in zip(cast_in, cast_out):
        dst[...] = src[...].astype(jnp.bfloat16)

    @pl.when(pl.program_id(0) == 0)
    def _():
        ubuf_ref[tm:tm + F32_SUBLANES, :] = jnp.zeros((F32_SUBLANES, CONV_WIDTH), jnp.float32)

    h = _rms(x_ref[...], g_ref[...]).astype(jnp.bfloat16)

    def proj(col):
        return _dot(h, w_ref[:, col * ATT_WIDTH:(col + 1) * ATT_WIDTH])

    def project_streams(col, out1_ref, out4_ref, out16_ref, scale=None):
        val = proj(col) if scale is None else proj(col) * scale
        out1_ref[...] = val.astype(jnp.bfloat16)
        for j in range(N_SLABS):
            nat_ref[j] = val[:, _slab(j)]
        _split_streams(nat_ref, out4_ref, out16_ref, by4_ref)

    project_streams(0, q1_ref, q4_ref, q16_ref, LOG2_E / math.sqrt(HEAD_DIM))
    project_streams(1, k1_ref, k4_ref, k16_ref)
    project_streams(2, v1_ref, v4_ref, v16_ref)

    seq_start = pl.program_id(0) % tiles_per_seq == 0
    ubuf_ref[0:F32_SUBLANES, :] = jnp.where(seq_start, 0.0, ubuf_ref[tm:tm + F32_SUBLANES, :])
    u = proj(4) * proj(5)
    ubuf_ref[F32_SUBLANES:tm + F32_SUBLANES, :] = u
    conv = (u * cw_ref[2:3, :]
            + ubuf_ref[F32_SUBLANES - 1:tm + F32_SUBLANES - 1, :] * cw_ref[1:2, :]
            + ubuf_ref[F32_SUBLANES - 2:tm + F32_SUBLANES - 2, :] * cw_ref[0:1, :])
    y = proj(3) * conv
    c_ref[...] = _rms(y, gc_ref[...]).astype(jnp.bfloat16)


def _stream_spec(dilation, tm, tiles_per_seq):
    return pl.BlockSpec((None, dilation, tm // dilation, ATT_WIDTH),
                        lambda i: (i // tiles_per_seq, 0, i % tiles_per_seq, 0))


def _in_proj(x2d, g_mix, w_in, conv_w, g_conv_out, later_weights, batch, seq_len):
    t = x2d.shape[0]
    tm = TOKEN_TILE
    assert seq_len % tm == 0 and t == batch * seq_len
    tiles_per_seq = seq_len // tm
    n_steps = t // tm
    nat = jax.ShapeDtypeStruct((t, ATT_WIDTH), jnp.bfloat16)
    by = lambda d: jax.ShapeDtypeStruct((batch, d, seq_len // d, ATT_WIDTH), jnp.bfloat16)
    row_spec = pl.BlockSpec((tm, ATT_WIDTH), lambda i: (i, 0))
    s4, s16 = _stream_spec(4, tm, tiles_per_seq), _stream_spec(16, tm, tiles_per_seq)
    assert all(w.shape[0] % (n_steps * BF16_SUBLANES) == 0 for w in later_weights)
    cast_specs = [pl.BlockSpec((w.shape[0] // n_steps, w.shape[1]), lambda i: (i, 0))
                  for w in later_weights]
    return pl.pallas_call(
        functools.partial(_in_proj_kernel, tiles_per_seq, len(later_weights)),
        grid=(n_steps,),
        in_specs=[
            pl.BlockSpec((tm, D_MODEL), lambda i: (i, 0)),
            _resident((1, D_MODEL)),
            _resident(w_in.shape),
            _resident(conv_w.shape),
            _resident((1, CONV_WIDTH)),
        ] + cast_specs,
        out_specs=[row_spec] * 3 + [s4] * 3 + [s16] * 3 + [row_spec] + cast_specs,
        out_shape=([nat] * 3 + [by(4)] * 3 + [by(16)] * 3 + [nat]
                   + [jax.ShapeDtypeStruct(w.shape, jnp.bfloat16) for w in later_weights]),
        scratch_shapes=[pltpu.VMEM((tm + F32_SUBLANES, CONV_WIDTH), jnp.float32),
                        pltpu.VMEM((N_SLABS, tm, LANES), jnp.float32),
                        pltpu.VMEM((N_SLABS, tm, LANES), jnp.float32)],
        compiler_params=_params(1),
        name="in_proj",
    )(x2d, g_mix.reshape(1, -1), w_in, conv_w, g_conv_out.reshape(1, -1), *later_weights)


def _attn_kernel(q_ref, kc_ref, kp_ref, vc_ref, vp_ref, o_ref, stat_ref):
    s = BAND_STEPS
    streams, row_blocks = q_ref.shape[0], q_ref.shape[1] // s
    has_prev = pl.program_id(2) > 0
    qi = jax.lax.broadcasted_iota(jnp.int32, (2 * s, 2 * s), 0) & (s - 1)
    kj = jax.lax.broadcasted_iota(jnp.int32, (2 * s, 2 * s), 1)
    band = (kj >= qi) & (kj <= qi + s)
    cap = jnp.where(band, jnp.inf, NEG_INF)
    cap_first = jnp.where(band & ((kj >= s) | has_prev), jnp.inf, NEG_INF)
    first_head = jax.lax.broadcasted_iota(jnp.int32, (s, PAIR_LANES), 1) < HEAD_DIM
    ones = jnp.ones((2 * s, PAIR_LANES), jnp.bfloat16)

    def rows_of(i):
        return slice(i * s, (i + 1) * s)

    def scores(r, i, pair):
        lanes = _slab(pair)
        q2 = q_ref[r, rows_of(i), lanes]
        zero = jnp.zeros_like(q2)
        q = jnp.concatenate([jnp.where(first_head, q2, zero),
                             jnp.where(first_head, zero, q2)], axis=0)
        k_prev = kp_ref[r, :, lanes] if i == 0 else kc_ref[r, rows_of(i - 1), lanes]
        k = jnp.concatenate([k_prev, kc_ref[r, rows_of(i), lanes]], axis=0)
        return jnp.minimum(_dot_nt(q, k), cap_first if i == 0 else cap)

    def finish(r, i, pair, sc):
        lanes = _slab(pair)
        v_prev = vp_ref[r, :, lanes] if i == 0 else vc_ref[r, rows_of(i - 1), lanes]
        v = jnp.concatenate(
            [jnp.concatenate([v_prev, vc_ref[r, rows_of(i), lanes]], axis=0), ones], axis=1)
        m = jnp.max(sc, axis=-1, keepdims=True)
        p = jnp.exp2(sc - m)
        pv = _dot(p.astype(jnp.bfloat16), v)
        o_ref[r, rows_of(i), lanes] = jnp.where(first_head, pv[:s, :PAIR_LANES],
                                                pv[s:, :PAIR_LANES])
        m_all = jnp.broadcast_to(m, (2 * s, PAIR_LANES))
        l_all = pv[:, PAIR_LANES:]
        for half in range(2):
            lo = (2 * pair + half) * STAT_LANES
            mid, hi = lo + STAT_LANES // 2, lo + STAT_LANES
            stat_ref[r, rows_of(i), lo:mid] = m_all[half * s:(half + 1) * s, lo:mid]
            stat_ref[r, rows_of(i), mid:hi] = l_all[half * s:(half + 1) * s, mid:hi]

    items = [(r, i, pair) for r in range(streams) for i in range(row_blocks)
             for pair in range(N_ATT_HEADS // 2)]
    sc = scores(*items[0])
    for n, item in enumerate(items):
        sc_next = scores(*items[n + 1]) if n + 1 < len(items) else None
        finish(*item, sc)
        sc = sc_next


def _dilated_attention(q, k, v):
    b, d, steps, _ = q.shape
    row_blocks = min(ATTN_SUBTILES, steps // BAND_STEPS)
    streams = ATTN_SUBTILES // row_blocks
    tile_rows = row_blocks * BAND_STEPS
    assert steps % tile_rows == 0 and d % streams == 0
    cur = pl.BlockSpec((None, streams, tile_rows, ATT_WIDTH), lambda bi, g, n: (bi, g, n, 0))
    prev = pl.BlockSpec((None, streams, BAND_STEPS, ATT_WIDTH),
                        lambda bi, g, n: (bi, g, jnp.maximum(n * row_blocks - 1, 0), 0))
    cur_lse = pl.BlockSpec((None, streams, tile_rows, LANES), lambda bi, g, n: (bi, g, n, 0))
    return pl.pallas_call(
        _attn_kernel,
        grid=(b, d // streams, steps // tile_rows),
        in_specs=[cur, cur, prev, cur, prev],
        out_specs=[cur, cur_lse],
        out_shape=[jax.ShapeDtypeStruct(q.shape, jnp.float32),
                   jax.ShapeDtypeStruct((b, d, steps, LANES), jnp.float32)],
        compiler_params=_params(3),
        name=f"dilated_attn_d{d}",
    )(q, k, k, v, v)


def _post_attn_kernel(o1_ref, l1_ref, o4_ref, l4_ref, o16_ref, l16_ref, c_ref, x_ref,
                      ga_ref, wout_ref, gx_ref, wq_ref, kv_ref, wo_ref,
                      gm_ref, wu_ref, wd_ref, gf_ref, y_ref,
                      on4_ref, ln4_ref, on16_ref, ln16_ref, by4_ref, xmid_ref):
    i = pl.program_id(0)

    @pl.when(i == 0)
    def _():
        xmid_ref[...] = jnp.zeros(xmid_ref.shape, jnp.float32)

    assert sum(MLP_CHUNKS_PER_STAGE) * FF_CHUNK == D_FF
    y = xmid_ref[(i + 1) % 2]
    h_mlp = _rms(y, gm_ref[...]).astype(jnp.bfloat16)
    chunk_ids = iter(range(D_FF // FF_CHUNK))

    def mlp_stage(y, stage):
        for _ in range(MLP_CHUNKS_PER_STAGE[stage]):
            c = next(chunk_ids)
            cols = slice(c * FF_CHUNK, (c + 1) * FF_CHUNK)
            a = jnp.maximum(_dot(h_mlp, wu_ref[:, cols]), 0.0)
            y = y + _dot((a * a).astype(jnp.bfloat16), wd_ref[cols, :])
        return y

    _merge_streams4(o4_ref, on4_ref)
    _merge_streams4(l4_ref, ln4_ref)
    _merge_streams16(o16_ref, by4_ref, on16_ref)
    _merge_streams16(l16_ref, by4_ref, ln16_ref)
    stats = [l1_ref[...], ln4_ref[0], ln16_ref[0]]
    lane = jax.lax.broadcasted_iota(jnp.int32, stats[0].shape, 1)
    top = jnp.maximum(jnp.maximum(stats[0], stats[1]), stats[2])
    scale = [jnp.exp2(st - top) for st in stats]
    sums = [jnp.take_along_axis(st, lane ^ (STAT_LANES // 2), axis=1) for st in stats]
    inv = 1.0 / (scale[0] * sums[0] + scale[1] * sums[1] + scale[2] * sums[2])
    weights = [sc * inv for sc in scale]
    slabs = []
    for j in range(N_SLABS):
        src = (2 * j + lane // HEAD_DIM) * STAT_LANES
        w = [jnp.take_along_axis(wp, src, axis=1) for wp in weights]
        slabs.append(w[0] * o1_ref[:, _slab(j)] + w[1] * on4_ref[j] + w[2] * on16_ref[j])
    attn = jnp.concatenate(slabs, axis=-1)
    attn_n = _rms(attn, ga_ref[...]).astype(jnp.bfloat16)
    y = mlp_stage(y, 0)
    x = (x_ref[...] + _dot(attn_n, wout_ref[0:ATT_WIDTH, :])
         + _dot(c_ref[...], wout_ref[ATT_WIDTH:, :]))

    y = mlp_stage(y, 1)

    h = _rms(x, gx_ref[...]).astype(jnp.bfloat16)
    q = (_dot(h, wq_ref[...]) * (LOG2_E / math.sqrt(MEM_HEAD_DIM))).astype(jnp.bfloat16)
    head = lambda hd: slice(hd * MEM_HEAD_DIM, (hd + 1) * MEM_HEAD_DIM)
    scores = [_dot_nt(q[:, head(hd)], kv_ref[:, head(hd)]) for hd in range(N_MEM_HEADS)]

    y = mlp_stage(y, 2)
    y_ref[...] = _rms(y, gf_ref[...])

    heads = []
    for hd in range(N_MEM_HEADS):
        sc = scores[hd]
        m = jnp.max(sc, axis=-1, keepdims=True)
        p = jnp.exp2(sc - m)
        l = jnp.sum(p, axis=-1, keepdims=True)
        v = kv_ref[:, D_MODEL + hd * MEM_HEAD_DIM:D_MODEL + (hd + 1) * MEM_HEAD_DIM]
        heads.append((_dot(p.astype(jnp.bfloat16), v) / l).astype(jnp.bfloat16))
    xmid_ref[i % 2] = x + _dot(jnp.concatenate(heads, axis=-1), wo_ref[...])


def _post_attn(o1, l1, o4, l4, o16, l16, convn, x2d, kv, g_attn_out, w_out, g_xattn, w_q, w_o,
               g_mlp, w_up, w_down, g_final, seq_len, mem_len):
    t = x2d.shape[0]
    tm = POST_TILE
    assert seq_len % tm == 0
    tiles_per_seq = seq_len // tm
    n_tiles = t // tm
    tile_in = lambda i: jnp.minimum(i, n_tiles - 1)
    row_spec = lambda width: pl.BlockSpec((tm, width), lambda i: (tile_in(i), 0))
    stream_spec = lambda d, width: pl.BlockSpec(
        (None, d, tm // d, width),
        lambda i: (tile_in(i) // tiles_per_seq, 0, tile_in(i) % tiles_per_seq, 0))
    x_spec = pl.BlockSpec((tm, D_MODEL), lambda i: (tile_in(i), 0))
    gain = lambda g: g.reshape(1, -1)
    nat = pltpu.VMEM((N_SLABS, tm, LANES), jnp.float32)
    nat_lse = pltpu.VMEM((1, tm, LANES), jnp.float32)
    return pl.pallas_call(
        _post_attn_kernel,
        grid=(n_tiles + 1,),
        in_specs=[row_spec(ATT_WIDTH), row_spec(LANES),
                  stream_spec(4, ATT_WIDTH), stream_spec(4, LANES),
                  stream_spec(16, ATT_WIDTH), stream_spec(16, LANES),
                  row_spec(ATT_WIDTH), x_spec,
                  _resident((1, ATT_WIDTH)), _resident(w_out.shape),
                  _resident((1, D_MODEL)), _resident(w_q.shape),
                  pl.BlockSpec((mem_len, 2 * D_MODEL),
                               lambda i: (tile_in(i) // tiles_per_seq, 0)),
                  _resident(w_o.shape),
                  _resident((1, D_MODEL)), _resident(w_up.shape), _resident(w_down.shape),
                  _resident((1, D_MODEL))],
        out_specs=pl.BlockSpec((tm, D_MODEL), lambda i: (jnp.maximum(i - 1, 0), 0)),
        out_shape=jax.ShapeDtypeStruct((t, D_MODEL), jnp.float32),
        scratch_shapes=[nat, nat_lse, nat, nat_lse, nat,
                        pltpu.VMEM((2, tm, D_MODEL), jnp.float32)],
        compiler_params=_params(1),
        name="post_attn",
    )(o1, l1, o4, l4, o16, l16, convn, x2d, gain(g_attn_out), w_out, gain(g_xattn), w_q, kv, w_o,
      gain(g_mlp), w_up, w_down, gain(g_final))


def _mem_kv_kernel(m_ref, g_ref, w_ref, win_ref, kv_ref, win_bf_ref):
    h = _rms(m_ref[...], g_ref[...]).astype(jnp.bfloat16)
    kv_ref[...] = _dot(h, w_ref[...].astype(jnp.bfloat16)).astype(jnp.bfloat16)
    win_bf_ref[...] = win_ref[...].astype(jnp.bfloat16)


def _mem_kv(mem2d, g_mem, w_kv, w_in):
    rows = mem2d.shape[0]
    tm = TOKEN_TILE
    n_steps = rows // tm
    assert w_in.shape[0] % (n_steps * BF16_SUBLANES) == 0
    win_spec = pl.BlockSpec((w_in.shape[0] // n_steps, w_in.shape[1]), lambda i: (i, 0))
    return pl.pallas_call(
        _mem_kv_kernel,
        grid=(n_steps,),
        in_specs=[pl.BlockSpec((tm, D_MODEL), lambda i: (i, 0)),
                  _resident((1, D_MODEL)), _resident(w_kv.shape), win_spec],
        out_specs=[pl.BlockSpec((tm, 2 * D_MODEL), lambda i: (i, 0)), win_spec],
        out_shape=[jax.ShapeDtypeStruct((rows, 2 * D_MODEL), jnp.bfloat16),
                   jax.ShapeDtypeStruct(w_in.shape, jnp.bfloat16)],
        compiler_params=_params(1),
        name="mem_kv_proj",
    )(mem2d, g_mem.reshape(1, -1), w_kv, w_in)


def kernel(x, mem, g_mix, w_in, conv_w, g_attn_out, g_conv_out, w_out, g_xattn, g_mem,
           w_q_mem, w_kv_mem, w_o_mem, g_mlp, w_up, w_down, g_final):
    b, seq, d = x.shape
    mem_len = mem.shape[1]
    assert tuple(dil for _, dil in DILATED_PATTERNS) == (1, 4, 16)
    assert all(win // dil == BAND_STEPS for win, dil in DILATED_PATTERNS)
    x2d = x.reshape(b * seq, d)

    kv, w_in_bf = _mem_kv(mem.reshape(b * mem_len, d), g_mem, w_kv_mem, w_in)
    (q1, k1, v1, q4, k4, v4, q16, k16, v16, convn,
     w_out_bf, w_q_bf, w_o_bf, w_up_bf, w_down_bf) = _in_proj(
        x2d, g_mix, w_in_bf, conv_w, g_conv_out,
        (w_out, w_q_mem, w_o_mem, w_up, w_down), b, seq)
    as_stream = lambda a: a.reshape(b, 1, seq, ATT_WIDTH)
    o1, l1 = _dilated_attention(as_stream(q1), as_stream(k1), as_stream(v1))
    o4, l4 = _dilated_attention(q4, k4, v4)
    o16, l16 = _dilated_attention(q16, k16, v16)
    flat = lambda a: a.reshape(b * seq, a.shape[-1])
    y = _post_attn(flat(o1), flat(l1), o4, l4, o16, l16, convn, x2d, kv,
                   g_attn_out, w_out_bf, g_xattn, w_q_bf, w_o_bf,
                   g_mlp, w_up_bf, w_down_bf, g_final, seq, mem_len)
    return y.reshape(b, seq, d)
```
